```python
import jax, jax.numpy as jnp
from jax import lax
import numpy as np

D_MODEL = 1024
BATCH = 8
SEQ = 2048
DEPTH = 2

N_META = 16
EPS = 1e-6
D_RNN = D_MODEL
RNN_BLOCKS = 4
RNN_BLOCK = D_RNN // RNN_BLOCKS
RNN_CONV = 4
LRU_C = 8.0
D_SC = D_MODEL
SC_CONV = 3
MLA_HEADS = 8
QK_NOPE = 128
QK_ROPE = 64
QK_HEAD = QK_NOPE + QK_ROPE
V_HEAD = 128
Q_LORA = 384
KV_LORA = 256
D_ATT = MLA_HEADS * V_HEAD
ROPE_THETA = 10000.0
Q_BLOCK = 128
N_BRANCH = 3
IN_SPLITS = (D_RNN, D_RNN, D_SC, D_SC, D_SC, D_SC, Q_LORA, KV_LORA, QK_ROPE, D_ATT, N_BRANCH * D_MODEL)
N_IN = sum(IN_SPLITS)

kernel_name = "hybrid_rglru_shortconv_mla_gated"


def rms_norm(x, g):
    xf = x.astype(jnp.float32)
    y = xf * lax.rsqrt(jnp.mean(xf * xf, axis=-1, keepdims=True) + EPS)
    return (y * g.astype(jnp.float32)).astype(x.dtype)


def causal_depthwise_conv(x, w, b=None):
    width, ch = w.shape
    y = lax.conv_general_dilated(
        x, w[:, None, :].astype(x.dtype), window_strides=(1,),
        padding=[(width - 1, 0)], dimension_numbers=("NWC", "WIO", "NWC"),
        feature_group_count=ch)
    return y if b is None else y + b.astype(x.dtype)


def _lin_rec_combine(left, right):
    a1, b1 = left
    a2, b2 = right
    return a1 * a2, a2 * b1 + b2


def rg_lru(x, conv_w, conv_b, wa, ba, wx, bx, lam):
    xc = causal_depthwise_conv(x, conv_w, conv_b)
    b_, t_, _ = xc.shape
    xb = xc.reshape(b_, t_, RNN_BLOCKS, RNN_BLOCK)
    r = jax.nn.sigmoid(jnp.einsum("btnd,nde->btne", xb, wa).reshape(b_, t_, D_RNN) + ba)
    i = jax.nn.sigmoid(jnp.einsum("btnd,nde->btne", xb, wx).reshape(b_, t_, D_RNN) + bx)
    log_a = -LRU_C * r.astype(jnp.float32) * jax.nn.softplus(-lam.astype(jnp.float32))
    a = jnp.exp(log_a)
    u = jnp.sqrt(-jnp.expm1(2.0 * log_a)) * (i * xc).astype(jnp.float32)
    _, h = lax.associative_scan(_lin_rec_combine, (a, u), axis=1)
    return h.astype(x.dtype)


def rope_tables(t):
    inv = ROPE_THETA ** (-jnp.arange(0, QK_ROPE, 2, dtype=jnp.float32) / QK_ROPE)
    ang = jnp.arange(t, dtype=jnp.float32)[:, None] * inv[None, :]
    return jnp.cos(ang), jnp.sin(ang)


def apply_rope(x, cos, sin):
    x1, x2 = jnp.split(x.astype(jnp.float32), 2, axis=-1)
    c = cos[None, :, None, :]
    s = sin[None, :, None, :]
    return jnp.concatenate([x1 * c - x2 * s, x2 * c + x1 * s], axis=-1).astype(x.dtype)


def causal_block_attention(q, k, v):
    b_, t_, h_, dq = q.shape
    nb = -(-t_ // Q_BLOCK)
    tp = nb * Q_BLOCK
    pad = ((0, 0), (0, tp - t_), (0, 0), (0, 0))
    q, k, v = jnp.pad(q, pad), jnp.pad(k, pad), jnp.pad(v, pad)
    qb = q.reshape(b_, nb, Q_BLOCK, h_, dq).transpose(1, 0, 2, 3, 4)
    k_pos = jnp.arange(tp)
    scale = dq ** -0.5

    def one_block(args):
        blk, q_blk = args
        s = jnp.einsum("bqhd,bkhd->bhqk", q_blk, k).astype(jnp.float32) * scale
        q_pos = blk * Q_BLOCK + jnp.arange(Q_BLOCK)
        mask = k_pos[None, :] <= q_pos[:, None]
        s = jnp.where(mask[None, None], s, jnp.float32(-1e30))
        p = jax.nn.softmax(s, axis=-1)
        return jnp.einsum("bhqk,bkhd->bqhd", p.astype(v.dtype), v)

    out = lax.map(one_block, (jnp.arange(nb), qb))
    return out.transpose(1, 0, 2, 3, 4).reshape(b_, tp, h_, V_HEAD)[:, :t_]


def mla(c_q, c_kv, k_rope, cq_g, w_uq, ckv_g, w_uk, w_uv, qn_g, kn_g, cos, sin):
    b_, t_, _ = c_q.shape
    q = (rms_norm(c_q, cq_g) @ w_uq).reshape(b_, t_, MLA_HEADS, QK_HEAD)
    ckv = rms_norm(c_kv, ckv_g)
    k_nope = (ckv @ w_uk).reshape(b_, t_, MLA_HEADS, QK_NOPE)
    v = (ckv @ w_uv).reshape(b_, t_, MLA_HEADS, V_HEAD)
    k_r = jnp.broadcast_to(k_rope[:, :, None, :], (b_, t_, MLA_HEADS, QK_ROPE))
    k = jnp.concatenate([k_nope, k_r], axis=-1)
    q = rms_norm(q, qn_g)
    k = rms_norm(k, kn_g)
    q = jnp.concatenate([q[..., :QK_NOPE], apply_rope(q[..., QK_NOPE:], cos, sin)], axis=-1)
    k = jnp.concatenate([k[..., :QK_NOPE], apply_rope(k[..., QK_NOPE:], cos, sin)], axis=-1)
    o = causal_block_attention(q, k, v)
    return o.reshape(b_, t_, D_ATT)


def hybrid_layer(x, norm_g, w_in, rg_conv_w, rg_conv_b, rg_wa, rg_ba, rg_wx, rg_bx, rg_lambda,
                 rg_out, sc_conv_w, sc_out, mla_cq_g, mla_w_uq, mla_ckv_g, mla_w_uk, mla_w_uv,
                 mla_qnorm_g, mla_knorm_g, mla_out, gate_b, w_out, cos, sin):
    b_, t_, _ = x.shape
    h = rms_norm(x, norm_g)
    z = h @ w_in
    idx = []
    acc = 0
    for s in IN_SPLITS[:-1]:
        acc += s
        idx.append(acc)
    (x_rnn, g_rnn, sc_b, sc_c, x_sc, g_sc, c_q, c_kv, k_rope, g_att, gate_logits) = jnp.split(z, idx, axis=-1)

    y_rnn = (rg_lru(x_rnn, rg_conv_w, rg_conv_b, rg_wa, rg_ba, rg_wx, rg_bx, rg_lambda)
             * jax.nn.silu(g_rnn)) @ rg_out
    y_sc = (sc_b * causal_depthwise_conv(sc_c * x_sc, sc_conv_w) * jax.nn.silu(g_sc)) @ sc_out
    y_att = (mla(c_q, c_kv, k_rope, mla_cq_g, mla_w_uq, mla_ckv_g, mla_w_uk, mla_w_uv,
                 mla_qnorm_g, mla_knorm_g, cos, sin) * jax.nn.silu(g_att)) @ mla_out

    gates = jax.nn.sigmoid(gate_logits + gate_b.reshape(-1)).reshape(b_, t_, N_BRANCH, D_MODEL)
    merged = gates[:, :, 0] * y_rnn + gates[:, :, 1] * y_sc + gates[:, :, 2] * y_att
    return x + merged @ w_out


def setup_inputs(seed: int = 0) -> dict:
    key = jax.random.key(seed)
    ks = jax.random.split(key, 32)

    def nrm(k, shape, scale):
        return jax.random.normal(k, shape, jnp.float32) * scale

    def gain(k, shape):
        return 1.0 + 0.05 * jax.random.normal(k, shape, jnp.float32)

    a8 = jax.random.uniform(ks[10], (DEPTH, D_RNN), jnp.float32, 0.9, 0.999)
    a_base = a8 ** (1.0 / LRU_C)
    rg_lambda = jnp.log(a_base) - jnp.log1p(-a_base)

    return {
        "x": nrm(ks[0], (BATCH, SEQ, D_MODEL), 1.0),
        "meta": nrm(ks[1], (N_META, D_MODEL), 1.0),
        "norm_g": gain(ks[2], (DEPTH, D_MODEL)),
        "w_in": nrm(ks[3], (DEPTH, D_MODEL, N_IN), D_MODEL ** -0.5),
        "rg_conv_w": nrm(ks[4], (DEPTH, RNN_CONV, D_RNN), RNN_CONV ** -0.5),
        "rg_conv_b": nrm(ks[5], (DEPTH, D_RNN), 0.02),
        "rg_wa": nrm(ks[6], (DEPTH, RNN_BLOCKS, RNN_BLOCK, RNN_BLOCK), RNN_BLOCK ** -0.5),
        "rg_ba": nrm(ks[7], (DEPTH, D_RNN), 0.02),
        "rg_wx": nrm(ks[8], (DEPTH, RNN_BLOCKS, RNN_BLOCK, RNN_BLOCK), RNN_BLOCK ** -0.5),
        "rg_bx": nrm(ks[9], (DEPTH, D_RNN), 0.02),
        "rg_lambda": rg_lambda,
        "rg_out": nrm(ks[11], (DEPTH, D_RNN, D_MODEL), D_RNN ** -0.5),
        "sc_conv_w": nrm(ks[12], (DEPTH, SC_CONV, D_SC), SC_CONV ** -0.5),
        "sc_out": nrm(ks[13], (DEPTH, D_SC, D_MODEL), D_SC ** -0.5),
        "mla_cq_g": gain(ks[14], (DEPTH, Q_LORA)),
        "mla_w_uq": nrm(ks[15], (DEPTH, Q_LORA, MLA_HEADS * QK_HEAD), Q_LORA ** -0.5),
        "mla_ckv_g": gain(ks[16], (DEPTH, KV_LORA)),
        "mla_w_uk": nrm(ks[17], (DEPTH, KV_LORA, MLA_HEADS * QK_NOPE), KV_LORA ** -0.5),
        "mla_w_uv": nrm(ks[18], (DEPTH, KV_LORA, MLA_HEADS * V_HEAD), KV_LORA ** -0.5),
        "mla_qnorm_g": gain(ks[19], (DEPTH, QK_HEAD)),
        "mla_knorm_g": gain(ks[20], (DEPTH, QK_HEAD)),
        "mla_out": nrm(ks[21], (DEPTH, D_ATT, D_MODEL), D_ATT ** -0.5),
        "gate_b": nrm(ks[22], (DEPTH, N_BRANCH, D_MODEL), 0.02),
        "w_out": nrm(ks[23], (DEPTH, D_MODEL, D_MODEL), D_MODEL ** -0.5),
    }


def reference(x, meta, norm_g, w_in, rg_conv_w, rg_conv_b, rg_wa, rg_ba, rg_wx, rg_bx, rg_lambda,
              rg_out, sc_conv_w, sc_out, mla_cq_g, mla_w_uq, mla_ckv_g, mla_w_uk, mla_w_uv,
              mla_qnorm_g, mla_knorm_g, mla_out, gate_b, w_out):
    b_ = x.shape[0]
    meta_b = jnp.broadcast_to(meta[None].astype(x.dtype), (b_, N_META, D_MODEL))
    h = jnp.concatenate([meta_b, x], axis=1)
    cos, sin = rope_tables(h.shape[1])
    for l in range(DEPTH):
        h = hybrid_layer(h, norm_g[l], w_in[l], rg_conv_w[l], rg_conv_b[l], rg_wa[l], rg_ba[l],
                         rg_wx[l], rg_bx[l], rg_lambda[l], rg_out[l], sc_conv_w[l], sc_out[l],
                         mla_cq_g[l], mla_w_uq[l], mla_ckv_g[l], mla_w_uk[l], mla_w_uv[l],
                         mla_qnorm_g[l], mla_knorm_g[l], mla_out[l], gate_b[l], w_out[l], cos, sin)
    return h[:, N_META:]
```

```python
import functools

import jax
import jax.numpy as jnp
from jax import lax
from jax.experimental import pallas as pl
from jax.experimental.pallas import tpu as pltpu

N_META = 16
EPS = 1e-6
RNN_BLOCKS = 4
RNN_CONV = 4
LRU_C = 8.0
SC_CONV = 3
MLA_HEADS = 8
QK_NOPE = 128
QK_ROPE = 64
QK_HEAD = QK_NOPE + QK_ROPE
V_HEAD = 128
Q_LORA = 384
KV_LORA = 256
ROPE_THETA = 10000.0

LANES = 128
HALO = 8
TIME_TILE = 256
ATTN_TILE = 256
VMEM_LIMIT = 56 * 1024 * 1024

F32 = jnp.float32
BF16 = jnp.bfloat16


def _rms(x, g):
    ms = jnp.mean(x * x, axis=-1, keepdims=True)
    return x * lax.rsqrt(ms + EPS) * g


def _silu(x):
    return x * jax.nn.sigmoid(x)


def _shifted_taps(buf, cur, taps, n_rows):
    width = len(taps)
    acc = taps[width - 1] * cur
    for k in range(width - 1):
        shift = width - 1 - k
        acc = acc + taps[k] * buf[pl.ds(HALO - shift, n_rows), :]
    return acc


def _rnn_kernel(x_ref, ng_ref, w_ref, cw_ref, cb_ref, wa_ref, ba_ref, wx_ref, bx_ref, lam_ref,
                wo_ref, y_ref, xbuf, a_buf, u_buf, h_buf, h_carry):
    tt = x_ref.shape[1]
    d_rnn = wo_ref.shape[0]
    blk = d_rnn // RNN_BLOCKS

    @pl.when(pl.program_id(1) == 0)
    def _():
        xbuf[pl.ds(0, HALO), :] = jnp.zeros((HALO, d_rnn), F32)
        h_carry[...] = jnp.zeros_like(h_carry)

    h = _rms(x_ref[0], ng_ref[...]).astype(BF16)
    z = jnp.dot(h, w_ref[...], preferred_element_type=F32)
    xr = z[:, :d_rnn]
    gate = z[:, d_rnn:]

    xbuf[pl.ds(HALO, tt), :] = xr
    taps = [cw_ref[pl.ds(k, 1), :] for k in range(RNN_CONV)]
    xc = _shifted_taps(xbuf, xr, taps, tt) + cb_ref[...]
    xbuf[pl.ds(0, HALO), :] = xbuf[pl.ds(tt, HALO), :]

    xcb = xc.astype(BF16)
    ra, ri = [], []
    for n in range(RNN_BLOCKS):
        xn = xcb[:, n * blk:(n + 1) * blk]
        ra.append(jnp.dot(xn, wa_ref[n], preferred_element_type=F32))
        ri.append(jnp.dot(xn, wx_ref[n], preferred_element_type=F32))
    r = jax.nn.sigmoid(jnp.concatenate(ra, axis=-1) + ba_ref[...])
    i = jax.nn.sigmoid(jnp.concatenate(ri, axis=-1) + bx_ref[...])

    lam = lam_ref[...]
    softplus_neg = jnp.maximum(-lam, 0.0) + jnp.log1p(jnp.exp(-jnp.abs(lam)))
    log_a = (-LRU_C) * r * softplus_neg
    a = jnp.exp(log_a)
    a_buf[...] = a
    u_buf[...] = jnp.sqrt(1.0 - a * a) * (i * xc)

    def step(t, hc):
        hc = a_buf[pl.ds(t, 1), :] * hc + u_buf[pl.ds(t, 1), :]
        h_buf[pl.ds(t, 1), :] = hc
        return hc

    h_carry[...] = lax.fori_loop(0, tt, step, h_carry[...], unroll=8)

    y = (h_buf[...] * _silu(gate)).astype(BF16)
    y_ref[0] = jnp.dot(y, wo_ref[...], preferred_element_type=F32).astype(y_ref.dtype)


def _sconv_kernel(x_ref, ng_ref, w_ref, cw_ref, wo_ref, y_ref, cbuf):
    tt = x_ref.shape[1]
    d_sc = wo_ref.shape[0]

    @pl.when(pl.program_id(1) == 0)
    def _():
        cbuf[pl.ds(0, HALO), :] = jnp.zeros((HALO, d_sc), F32)

    h = _rms(x_ref[0], ng_ref[...]).astype(BF16)
    z = jnp.dot(h, w_ref[...], preferred_element_type=F32)
    sc_b = z[:, :d_sc]
    cx = z[:, d_sc:2 * d_sc] * z[:, 2 * d_sc:3 * d_sc]
    gate = z[:, 3 * d_sc:]

    cbuf[pl.ds(HALO, tt), :] = cx
    taps = [cw_ref[pl.ds(k, 1), :] for k in range(SC_CONV)]
    conv = _shifted_taps(cbuf, cx, taps, tt)
    cbuf[pl.ds(0, HALO), :] = cbuf[pl.ds(tt, HALO), :]

    y = (sc_b * conv * _silu(gate)).astype(BF16)
    y_ref[0] = jnp.dot(y, wo_ref[...], preferred_element_type=F32).astype(y_ref.dtype)


def _rope(x, cos_t, sin_t):
    half = QK_ROPE // 2
    left = pltpu.roll(x, half, axis=1)
    right = pltpu.roll(x, LANES - half, axis=1)
    return x * cos_t + (left - right) * sin_t


def _qkv_kernel(x_ref, ng_ref, w_ref, cqg_ref, ckvg_ref, wq_ref, wkv_ref, qgn_ref, qgr_ref,
                kgn_ref, kgr_ref, cos_ref, sin_ref, q_ref, k_ref, v_ref):
    d_nope = MLA_HEADS * QK_NOPE
    scale = QK_HEAD ** -0.5
    inv_head = 1.0 / QK_HEAD

    h = _rms(x_ref[0], ng_ref[...]).astype(BF16)
    z = jnp.dot(h, w_ref[...], preferred_element_type=F32)
    cq = _rms(z[:, :Q_LORA], cqg_ref[...]).astype(BF16)
    ckv = _rms(z[:, Q_LORA:Q_LORA + KV_LORA], ckvg_ref[...]).astype(BF16)
    kr = z[:, Q_LORA + KV_LORA:]

    q2 = jnp.dot(cq, wq_ref[...], preferred_element_type=F32)
    kv = jnp.dot(ckv, wkv_ref[...], preferred_element_type=F32)

    cos_t = cos_ref[...]
    sin_t = sin_ref[...]
    kr_ss = jnp.sum(kr * kr, axis=-1, keepdims=True)
    kr_rot = _rope(kr * kgr_ref[...], cos_t, sin_t)

    for hd in range(MLA_HEADS):
        qn = q2[:, hd * QK_NOPE:(hd + 1) * QK_NOPE]
        qr = q2[:, d_nope + hd * LANES:d_nope + (hd + 1) * LANES]
        q_ss = jnp.sum(qn * qn, axis=-1, keepdims=True) + jnp.sum(qr * qr, axis=-1, keepdims=True)
        q_inv = lax.rsqrt(q_ss * inv_head + EPS) * scale
        q_ref[0, hd, :, pl.ds(0, QK_NOPE)] = (qn * q_inv * qgn_ref[...]).astype(q_ref.dtype)
        q_ref[0, hd, :, pl.ds(QK_NOPE, LANES)] = (
            _rope(qr * qgr_ref[...], cos_t, sin_t) * q_inv).astype(q_ref.dtype)

        kn = kv[:, hd * QK_NOPE:(hd + 1) * QK_NOPE]
        k_ss = jnp.sum(kn * kn, axis=-1, keepdims=True) + kr_ss
        k_inv = lax.rsqrt(k_ss * inv_head + EPS)
        k_ref[0, hd, :, pl.ds(0, QK_NOPE)] = (kn * k_inv * kgn_ref[...]).astype(k_ref.dtype)
        k_ref[0, hd, :, pl.ds(QK_NOPE, LANES)] = (kr_rot * k_inv).astype(k_ref.dtype)

        v_ref[0, hd] = kv[:, d_nope + hd * V_HEAD:d_nope + (hd + 1) * V_HEAD].astype(v_ref.dtype)


def _attn_kernel(q_ref, k_ref, v_ref, o_ref):
    tq = q_ref.shape[2]
    qi = pl.program_id(2)
    q = q_ref[0, 0]

    def scores(j):
        k = k_ref[0, 0, pl.ds(pl.multiple_of(j * tq, tq), tq), :]
        return lax.dot_general(q, k, (((1,), (1,)), ((), ())), preferred_element_type=F32)

    def update(j, s, carry):
        m, l, acc = carry
        v = v_ref[0, 0, pl.ds(pl.multiple_of(j * tq, tq), tq), :]
        m_new = jnp.maximum(m, jnp.max(s, axis=-1, keepdims=True))
        alpha = jnp.exp(m - m_new)
        p = jnp.exp(s - m_new)
        l = alpha * l + jnp.sum(p, axis=-1, keepdims=True)
        acc = alpha * acc + jnp.dot(p.astype(BF16), v, preferred_element_type=F32)
        return m_new, l, acc

    init = (jnp.full((tq, 1), -1e30, F32), jnp.zeros((tq, 1), F32), jnp.zeros((tq, V_HEAD), F32))
    carry = lax.fori_loop(0, qi, lambda j, c: update(j, scores(j), c), init)

    row = lax.broadcasted_iota(jnp.int32, (tq, tq), 0)
    col = lax.broadcasted_iota(jnp.int32, (tq, tq), 1)
    s = jnp.where(col <= row, scores(qi), -1e30)
    _, l, acc = update(qi, s, carry)
    o_ref[0] = (acc / l).astype(o_ref.dtype)


def _merge_kernel(x_ref, yr_ref, ys_ref, o_ref, ng_ref, w_ref, gb_ref, wao_ref, wout_ref, out_ref):
    d = x_ref.shape[2]
    x = x_ref[0]
    h = _rms(x, ng_ref[...]).astype(BF16)
    z = jnp.dot(h, w_ref[...], preferred_element_type=F32)
    att = (o_ref[0].astype(F32) * _silu(z[:, :d])).astype(BF16)
    y_att = jnp.dot(att, wao_ref[...], preferred_element_type=F32)
    gates = jax.nn.sigmoid(z[:, d:] + gb_ref[...])
    merged = (gates[:, :d] * yr_ref[0].astype(F32) + gates[:, d:2 * d] * ys_ref[0].astype(F32)
              + gates[:, 2 * d:] * y_att)
    out_ref[0] = x + jnp.dot(merged.astype(BF16), wout_ref[...], preferred_element_type=F32)


def _const_spec(shape):
    nd = len(shape)
    return pl.BlockSpec(shape, lambda *_: (0,) * nd)


def _row_spec(tt, width):
    return pl.BlockSpec((1, tt, width), lambda b, t: (b, t, 0))


def _params(semantics):
    return pltpu.CompilerParams(dimension_semantics=semantics, vmem_limit_bytes=VMEM_LIMIT)


def _rnn_call(h, ng, w, cw, cb, wa, ba, wx, bx, lam, wo):
    b, tp, d = h.shape
    d_rnn = wo.shape[0]
    tt = TIME_TILE
    consts = (ng, w, cw, cb, wa, ba, wx, bx, lam, wo)
    return pl.pallas_call(
        _rnn_kernel,
        grid=(b, tp // tt),
        in_specs=[_row_spec(tt, d)] + [_const_spec(c.shape) for c in consts],
        out_specs=_row_spec(tt, d),
        out_shape=jax.ShapeDtypeStruct((b, tp, d), BF16),
        scratch_shapes=[pltpu.VMEM((tt + HALO, d_rnn), F32), pltpu.VMEM((tt, d_rnn), F32),
                        pltpu.VMEM((tt, d_rnn), F32), pltpu.VMEM((tt, d_rnn), F32),
                        pltpu.VMEM((1, d_rnn), F32)],
        compiler_params=_params(("arbitrary", "arbitrary")),
        name="rnn_branch",
    )(h, *consts)


def _sconv_call(h, ng, w, cw, wo):
    b, tp, d = h.shape
    d_sc = wo.shape[0]
    tt = TIME_TILE
    consts = (ng, w, cw, wo)
    return pl.pallas_call(
        _sconv_kernel,
        grid=(b, tp // tt),
        in_specs=[_row_spec(tt, d)] + [_const_spec(c.shape) for c in consts],
        out_specs=_row_spec(tt, d),
        out_shape=jax.ShapeDtypeStruct((b, tp, d), BF16),
        scratch_shapes=[pltpu.VMEM((tt + HALO, d_sc), F32)],
        compiler_params=_params(("arbitrary", "arbitrary")),
        name="sconv_branch",
    )(h, *consts)


def _qkv_call(h, ng, w, cqg, ckvg, wq, wkv, qgn, qgr, kgn, kgr, cos_t, sin_t):
    b, tp, d = h.shape
    tt = TIME_TILE
    consts = (ng, w, cqg, ckvg, wq, wkv, qgn, qgr, kgn, kgr)
    tab_spec = pl.BlockSpec((tt, LANES), lambda bi, t: (t, 0))
    qk_spec = pl.BlockSpec((1, MLA_HEADS, tt, 2 * LANES), lambda bi, t: (bi, 0, t, 0))
    v_spec = pl.BlockSpec((1, MLA_HEADS, tt, V_HEAD), lambda bi, t: (bi, 0, t, 0))
    return pl.pallas_call(
        _qkv_kernel,
        grid=(b, tp // tt),
        in_specs=[_row_spec(tt, d)] + [_const_spec(c.shape) for c in consts] + [tab_spec, tab_spec],
        out_specs=[qk_spec, qk_spec, v_spec],
        out_shape=[jax.ShapeDtypeStruct((b, MLA_HEADS, tp, 2 * LANES), BF16),
                   jax.ShapeDtypeStruct((b, MLA_HEADS, tp, 2 * LANES), BF16),
                   jax.ShapeDtypeStruct((b, MLA_HEADS, tp, V_HEAD), BF16)],
        compiler_params=_params(("arbitrary", "arbitrary")),
        name="mla_qkv",
    )(h, *consts, cos_t, sin_t)


def _attn_call(q, k, v):
    b, nh, tp, dk = q.shape
    tq = ATTN_TILE
    return pl.pallas_call(
        _attn_kernel,
        grid=(b, nh, tp // tq),
        in_specs=[pl.BlockSpec((1, 1, tq, dk), lambda bi, hi, qi: (bi, hi, qi, 0)),
                  pl.BlockSpec((1, 1, tp, dk), lambda bi, hi, qi: (bi, hi, 0, 0)),
                  pl.BlockSpec((1, 1, tp, V_HEAD), lambda bi, hi, qi: (bi, hi, 0, 0))],
        out_specs=pl.BlockSpec((1, tq, V_HEAD), lambda bi, hi, qi: (bi, qi, hi)),
        out_shape=jax.ShapeDtypeStruct((b, tp, nh * V_HEAD), BF16),
        compiler_params=_params(("arbitrary", "arbitrary", "arbitrary")),
        name="mla_attention",
    )(q, k, v)


def _merge_call(h, y_rnn, y_sc, o, ng, w, gb, wao, wout):
    b, tp, d = h.shape
    tt = TIME_TILE
    consts = (ng, w, gb, wao, wout)
    return pl.pallas_call(
        _merge_kernel,
        grid=(b, tp // tt),
        in_specs=[_row_spec(tt, d)] * 4 + [_const_spec(c.shape) for c in consts],
        out_specs=_row_spec(tt, d),
        out_shape=jax.ShapeDtypeStruct((b, tp, d), F32),
        compiler_params=_params(("arbitrary", "arbitrary")),
        name="gated_merge",
    )(h, y_rnn, y_sc, o, *consts)


def _rope_tables(tp):
    half = QK_ROPE // 2
    inv = ROPE_THETA ** (-jnp.arange(0, QK_ROPE, 2, dtype=F32) / QK_ROPE)
    ang = jnp.arange(tp, dtype=F32)[:, None] * inv[None, :]
    cos, sin = jnp.cos(ang), jnp.sin(ang)
    zeros = jnp.zeros((tp, LANES - 2 * half), F32)
    return jnp.tile(cos, (1, LANES // half)), jnp.concatenate([sin, sin, zeros], axis=1)


def _pad_lanes(a, width):
    return jnp.pad(a, [(0, 0)] * (a.ndim - 1) + [(0, width - a.shape[-1])])


def kernel(x, meta, norm_g, w_in, rg_conv_w, rg_conv_b, rg_wa, rg_ba, rg_wx, rg_bx, rg_lambda, rg_out,
           sc_conv_w, sc_out, mla_cq_g, mla_w_uq, mla_ckv_g, mla_w_uk, mla_w_uv, mla_qnorm_g,
           mla_knorm_g, mla_out, gate_b, w_out):
    b, seq, d = x.shape
    depth = norm_g.shape[0]
    d_rnn = rg_out.shape[1]
    d_sc = sc_out.shape[1]
    t_real = N_META + seq
    tp = -(-t_real // TIME_TILE) * TIME_TILE

    meta_b = jnp.broadcast_to(meta[None].astype(x.dtype), (b, N_META, d))
    h = jnp.concatenate([meta_b, x, jnp.zeros((b, tp - t_real, d), x.dtype)], axis=1)
    cos_t, sin_t = _rope_tables(tp)

    o_sc = 2 * d_rnn
    o_cq = o_sc + 4 * d_sc
    o_kr = o_cq + Q_LORA + KV_LORA
    o_ga = o_kr + QK_ROPE

    row = lambda a: a.reshape(1, -1)
    for l in range(depth):
        wl = w_in[l]
        ng = row(norm_g[l])
        w_rnn = wl[:, :o_sc].astype(BF16)
        w_sc = wl[:, o_sc:o_cq].astype(BF16)
        w_lat = jnp.concatenate([wl[:, o_cq:o_kr], _pad_lanes(wl[:, o_kr:o_ga], LANES)], axis=1).astype(BF16)
        w_mg = wl[:, o_ga:].astype(BF16)

        wq = mla_w_uq[l].reshape(Q_LORA, MLA_HEADS, QK_HEAD)
        wq_all = jnp.concatenate(
            [wq[:, :, :QK_NOPE].reshape(Q_LORA, -1), _pad_lanes(wq[:, :, QK_NOPE:], LANES).reshape(Q_LORA, -1)],
            axis=1).astype(BF16)
        wkv = jnp.concatenate([mla_w_uk[l], mla_w_uv[l]], axis=1).astype(BF16)
        qg, kg = mla_qnorm_g[l], mla_knorm_g[l]

        y_rnn = _rnn_call(h, ng, w_rnn, rg_conv_w[l], row(rg_conv_b[l]), rg_wa[l].astype(BF16),
                          row(rg_ba[l]), rg_wx[l].astype(BF16), row(rg_bx[l]), row(rg_lambda[l]),
                          rg_out[l].astype(BF16))
        y_sc = _sconv_call(h, ng, w_sc, sc_conv_w[l], sc_out[l].astype(BF16))
        q, k, v = _qkv_call(h, ng, w_lat, row(mla_cq_g[l]), row(mla_ckv_g[l]), wq_all, wkv,
                            row(qg[:QK_NOPE]), _pad_lanes(row(qg[QK_NOPE:]), LANES),
                            row(kg[:QK_NOPE]), _pad_lanes(row(kg[QK_NOPE:]), LANES), cos_t, sin_t)
        o = _attn_call(q, k, v)
        h = _merge_call(h, y_rnn, y_sc, o, ng, w_mg, row(gate_b[l]), mla_out[l].astype(BF16),
                        w_out[l].astype(BF16))
    return h[:, N_META:t_real]
```

```python
import jax
import jax.numpy as jnp
from jax import lax
from jax.experimental import pallas as pl
from jax.experimental.pallas import tpu as pltpu

N_META = 16
EPS = 1e-6
RNN_BLOCKS = 4
RNN_CONV = 4
LRU_C = 8.0
SC_CONV = 3
MLA_HEADS = 8
QK_NOPE = 128
QK_ROPE = 64
QK_HEAD = QK_NOPE + QK_ROPE
V_HEAD = 128
Q_LORA = 384
KV_LORA = 256
ROPE_THETA = 10000.0

LANES = 128
SUBLANES = 8
STEPS_PER_TILE = 48
QKV_ROWS = 688
ATTN_TILE = 256
HEAD_GROUP = 2
VMEM_LIMIT = 56 * 1024 * 1024

F32 = jnp.float32
BF16 = jnp.bfloat16


def _rms(x, g):
    ms = jnp.mean(x * x, axis=-1, keepdims=True)
    return x * lax.rsqrt(ms + EPS) * g


def _silu(x):
    return x * jax.nn.sigmoid(x)


def _causal_taps(buf, cur, taps, halo, rows):
    width = len(taps)
    acc = taps[width - 1] * cur
    for k in range(width - 1):
        back = (width - 1 - k) * SUBLANES
        acc = acc + taps[k] * buf[pl.ds(halo - back, rows), :]
    return acc


def _rnn_kernel(x_ref, ng_ref, w_ref, cw_ref, cb_ref, wa_ref, ba_ref, wx_ref, bx_ref, lam_ref,
                wo_ref, y_ref, xbuf, h_buf, h_carry):
    rows = x_ref.shape[0]
    d_rnn = wo_ref.shape[0]
    blk = d_rnn // RNN_BLOCKS
    halo = (RNN_CONV - 1) * SUBLANES

    @pl.when(pl.program_id(0) == 0)
    def _():
        xbuf[pl.ds(0, halo), :] = jnp.zeros((halo, d_rnn), F32)
        h_carry[...] = jnp.zeros_like(h_carry)

    h = _rms(x_ref[...], ng_ref[...]).astype(BF16)
    z = jnp.dot(h, w_ref[...], preferred_element_type=F32)
    xr = z[:, :d_rnn]
    gate = z[:, d_rnn:]

    xbuf[pl.ds(halo, rows), :] = xr
    taps = [cw_ref[pl.ds(k, 1), :] for k in range(RNN_CONV)]
    xc = _causal_taps(xbuf, xr, taps, halo, rows) + cb_ref[...]
    xbuf[pl.ds(0, halo), :] = xbuf[pl.ds(rows, halo), :]

    xcb = xc.astype(BF16)
    ra, ri = [], []
    for n in range(RNN_BLOCKS):
        xn = xcb[:, n * blk:(n + 1) * blk]
        ra.append(jnp.dot(xn, wa_ref[n], preferred_element_type=F32))
        ri.append(jnp.dot(xn, wx_ref[n], preferred_element_type=F32))
    r = jax.nn.sigmoid(jnp.concatenate(ra, axis=-1) + ba_ref[...])
    i = jax.nn.sigmoid(jnp.concatenate(ri, axis=-1) + bx_ref[...])

    lam = lam_ref[...]
    softplus_neg = jnp.maximum(-lam, 0.0) + jnp.log1p(jnp.exp(-jnp.abs(lam)))
    a = jnp.exp((-LRU_C) * r * softplus_neg)
    u = jnp.sqrt(1.0 - a * a) * (i * xc)

    hc = h_carry[...]
    for t in range(rows // SUBLANES):
        sl = slice(t * SUBLANES, (t + 1) * SUBLANES)
        hc = a[sl] * hc + u[sl]
        h_buf[sl, :] = hc
    h_carry[...] = hc

    y = (h_buf[...] * _silu(gate)).astype(BF16)
    y_ref[...] = jnp.dot(y, wo_ref[...], preferred_element_type=F32).astype(y_ref.dtype)


def _sconv_kernel(x_ref, ng_ref, w_ref, cw_ref, wo_ref, y_ref, cbuf):
    rows = x_ref.shape[0]
    d_sc = wo_ref.shape[0]
    halo = (SC_CONV - 1) * SUBLANES

    @pl.when(pl.program_id(0) == 0)
    def _():
        cbuf[pl.ds(0, halo), :] = jnp.zeros((halo, d_sc), F32)

    h = _rms(x_ref[...], ng_ref[...]).astype(BF16)
    z = jnp.dot(h, w_ref[...], preferred_element_type=F32)
    sc_b = z[:, :d_sc]
    cx = z[:, d_sc:2 * d_sc] * z[:, 2 * d_sc:3 * d_sc]
    gate = z[:, 3 * d_sc:]

    cbuf[pl.ds(halo, rows), :] = cx
    taps = [cw_ref[pl.ds(k, 1), :] for k in range(SC_CONV)]
    conv = _causal_taps(cbuf, cx, taps, halo, rows)
    cbuf[pl.ds(0, halo), :] = cbuf[pl.ds(rows, halo), :]

    y = (sc_b * conv * _silu(gate)).astype(BF16)
    y_ref[...] = jnp.dot(y, wo_ref[...], preferred_element_type=F32).astype(y_ref.dtype)


def _qkv_kernel(x_ref, ng_ref, w_ref, cqg_ref, ckvg_ref, wq_ref, wkv_ref, qgn_ref, qga_ref, qgb_ref,
                kgn_ref, kgr_ref, cos_ref, sin_ref, q_ref, k_ref, v_ref):
    scale = QK_HEAD ** -0.5
    inv_head = 1.0 / QK_HEAD
    half = QK_ROPE // 2

    h = _rms(x_ref[...], ng_ref[...]).astype(BF16)
    z = jnp.dot(h, w_ref[...], preferred_element_type=F32)
    cq = _rms(z[:, :Q_LORA], cqg_ref[...]).astype(BF16)
    ckv = _rms(z[:, Q_LORA:Q_LORA + KV_LORA], ckvg_ref[...]).astype(BF16)
    kr = z[:, Q_LORA + KV_LORA:]

    cos_t = cos_ref[...]
    sin_t = sin_ref[...]
    kr_sq = kr * kr
    krg = kr * kgr_ref[...]
    kr_rot = krg * cos_t + (pltpu.roll(krg, half, axis=1) - pltpu.roll(krg, LANES - half, axis=1)) * sin_t

    for grp in range(MLA_HEADS // HEAD_GROUP):
        qg = jnp.dot(cq, wq_ref[grp], preferred_element_type=F32)
        kvg = jnp.dot(ckv, wkv_ref[grp], preferred_element_type=F32)
        for j in range(HEAD_GROUP):
            hd = grp * HEAD_GROUP + j
            qn = qg[:, (3 * j) * LANES:(3 * j + 1) * LANES]
            qa = qg[:, (3 * j + 1) * LANES:(3 * j + 2) * LANES]
            qb = qg[:, (3 * j + 2) * LANES:(3 * j + 3) * LANES]
            q_ss = jnp.sum(qn * qn + qa * qa, axis=-1, keepdims=True)
            q_inv = lax.rsqrt(q_ss * inv_head + EPS) * scale
            q_rot = qa * qga_ref[...] * cos_t + qb * qgb_ref[...] * sin_t
            q_ref[0, hd, :, pl.ds(0, QK_NOPE)] = (qn * q_inv * qgn_ref[...]).astype(q_ref.dtype)
            q_ref[0, hd, :, pl.ds(QK_NOPE, LANES)] = (q_rot * q_inv).astype(q_ref.dtype)

            kn = kvg[:, (2 * j) * LANES:(2 * j + 1) * LANES]
            k_ss = jnp.sum(kn * kn + kr_sq, axis=-1, keepdims=True)
            k_inv = lax.rsqrt(k_ss * inv_head + EPS)
            k_ref[0, hd, :, pl.ds(0, QK_NOPE)] = (kn * k_inv * kgn_ref[...]).astype(k_ref.dtype)
            k_ref[0, hd, :, pl.ds(QK_NOPE, LANES)] = (kr_rot * k_inv).astype(k_ref.dtype)
            v_ref[0, hd] = kvg[:, (2 * j + 1) * LANES:(2 * j + 2) * LANES].astype(v_ref.dtype)


def _attn_kernel(q_ref, k_ref, v_ref, o_ref):
    t_all = q_ref.shape[2]
    nt = (((1,), (1,)), ((), ()))

    def tile(start, size):
        q = q_ref[0, 0, pl.ds(start, size), :]
        s_dg = lax.dot_general(q, k_ref[0, 0, pl.ds(start, size), :], nt, preferred_element_type=F32)
        row = lax.broadcasted_iota(jnp.int32, (size, size), 0)
        col = lax.broadcasted_iota(jnp.int32, (size, size), 1)
        s_dg = jnp.where(col <= row, s_dg, -1e30)
        m = jnp.max(s_dg, axis=-1, keepdims=True)
        if start:
            s_off = lax.dot_general(q, k_ref[0, 0, pl.ds(0, start), :], nt, preferred_element_type=F32)
            m = jnp.maximum(m, jnp.max(s_off, axis=-1, keepdims=True))
        p_dg = jnp.exp(s_dg - m)
        l = jnp.sum(p_dg, axis=-1, keepdims=True)
        acc = jnp.dot(p_dg.astype(BF16), v_ref[0, 0, pl.ds(start, size), :], preferred_element_type=F32)
        if start:
            p_off = jnp.exp(s_off - m)
            l = l + jnp.sum(p_off, axis=-1, keepdims=True)
            acc = acc + jnp.dot(p_off.astype(BF16), v_ref[0, 0, pl.ds(0, start), :],
                                preferred_element_type=F32)
        o_ref[pl.ds(start, size), :] = (acc / l).astype(o_ref.dtype)

    n_full = t_all // ATTN_TILE
    for i in range(n_full):
        tile(i * ATTN_TILE, ATTN_TILE)
    if t_all % ATTN_TILE:
        tile(n_full * ATTN_TILE, t_all % ATTN_TILE)


def _merge_kernel(x_ref, yr_ref, ys_ref, o_ref, ng_ref, w_ref, gb_ref, wao_ref, wout_ref, out_ref):
    d = x_ref.shape[1]
    x = x_ref[...]
    h = _rms(x, ng_ref[...]).astype(BF16)
    z = jnp.dot(h, w_ref[...], preferred_element_type=F32)
    att = (o_ref[...].astype(F32) * _silu(z[:, :d])).astype(BF16)
    y_att = jnp.dot(att, wao_ref[...], preferred_element_type=F32)
    gates = jax.nn.sigmoid(z[:, d:] + gb_ref[...])
    merged = (gates[:, :d] * yr_ref[...].astype(F32) + gates[:, d:2 * d] * ys_ref[...].astype(F32)
              + gates[:, 2 * d:] * y_att)
    out_ref[...] = x + jnp.dot(merged.astype(BF16), wout_ref[...], preferred_element_type=F32)


def _const_spec(shape):
    nd = len(shape)
    return pl.BlockSpec(shape, lambda *_: (0,) * nd)


def _params(n_axes):
    return pltpu.CompilerParams(dimension_semantics=("arbitrary",) * n_axes, vmem_limit_bytes=VMEM_LIMIT)


def _rnn_call(h, ng, w, cw, cb, wa, ba, wx, bx, lam, wo):
    n_rows, d = h.shape
    d_rnn = wo.shape[0]
    rows = STEPS_PER_TILE * SUBLANES
    consts = (ng, w, cw, cb, wa, ba, wx, bx, lam, wo)
    row_spec = pl.BlockSpec((rows, d), lambda t: (t, 0))
    return pl.pallas_call(
        _rnn_kernel,
        grid=(n_rows // rows,),
        in_specs=[row_spec] + [_const_spec(c.shape) for c in consts],
        out_specs=row_spec,
        out_shape=jax.ShapeDtypeStruct((n_rows, d), BF16),
        scratch_shapes=[pltpu.VMEM((rows + (RNN_CONV - 1) * SUBLANES, d_rnn), F32),
                        pltpu.VMEM((rows, d_rnn), F32), pltpu.VMEM((SUBLANES, d_rnn), F32)],
        compiler_params=_params(1),
        name="rnn_branch",
    )(h, *consts)


def _sconv_call(h, ng, w, cw, wo):
    n_rows, d = h.shape
    d_sc = wo.shape[0]
    rows = STEPS_PER_TILE * SUBLANES
    consts = (ng, w, cw, wo)
    row_spec = pl.BlockSpec((rows, d), lambda t: (t, 0))
    return pl.pallas_call(
        _sconv_kernel,
        grid=(n_rows // rows,),
        in_specs=[row_spec] + [_const_spec(c.shape) for c in consts],
        out_specs=row_spec,
        out_shape=jax.ShapeDtypeStruct((n_rows, d), BF16),
        scratch_shapes=[pltpu.VMEM((rows + (SC_CONV - 1) * SUBLANES, d_sc), F32)],
        compiler_params=_params(1),
        name="sconv_branch",
    )(h, *consts)


def _qkv_call(h_tb, d, ng, w, cqg, ckvg, wq, wkv, qgn, qga, qgb, kgn, kgr, cos_t, sin_t):
    t_all, bd = h_tb.shape
    b = bd // d
    rows = QKV_ROWS
    consts = (ng, w, cqg, ckvg, wq, wkv, qgn, qga, qgb, kgn, kgr)
    tab_spec = pl.BlockSpec((rows, LANES), lambda bi, t: (t, 0))
    qk_spec = pl.BlockSpec((1, MLA_HEADS, rows, 2 * LANES), lambda bi, t: (bi, 0, t, 0))
    v_spec = pl.BlockSpec((1, MLA_HEADS, rows, V_HEAD), lambda bi, t: (bi, 0, t, 0))
    return pl.pallas_call(
        _qkv_kernel,
        grid=(b, t_all // rows),
        in_specs=[pl.BlockSpec((rows, d), lambda bi, t: (t, bi))] + [_const_spec(c.shape) for c in consts]
        + [tab_spec, tab_spec],
        out_specs=[qk_spec, qk_spec, v_spec],
        out_shape=[jax.ShapeDtypeStruct((b, MLA_HEADS, t_all, 2 * LANES), BF16),
                   jax.ShapeDtypeStruct((b, MLA_HEADS, t_all, 2 * LANES), BF16),
                   jax.ShapeDtypeStruct((b, MLA_HEADS, t_all, V_HEAD), BF16)],
        compiler_params=_params(2),
        name="mla_qkv",
    )(h_tb, *consts, cos_t, sin_t)


def _attn_call(q, k, v):
    b, nh, t_all, dk = q.shape
    return pl.pallas_call(
        _attn_kernel,
        grid=(b, nh),
        in_specs=[pl.BlockSpec((1, 1, t_all, dk), lambda bi, hi: (bi, hi, 0, 0)),
                  pl.BlockSpec((1, 1, t_all, dk), lambda bi, hi: (bi, hi, 0, 0)),
                  pl.BlockSpec((1, 1, t_all, V_HEAD), lambda bi, hi: (bi, hi, 0, 0))],
        out_specs=pl.BlockSpec((t_all, V_HEAD), lambda bi, hi: (0, bi * nh + hi)),
        out_shape=jax.ShapeDtypeStruct((t_all, b * nh * V_HEAD), BF16),
        compiler_params=_params(2),
        name="mla_attention",
    )(q, k, v)


def _merge_call(h, y_rnn, y_sc, o, ng, w, gb, wao, wout):
    n_rows, d = h.shape
    rows = STEPS_PER_TILE * SUBLANES
    consts = (ng, w, gb, wao, wout)
    row_spec = pl.BlockSpec((rows, d), lambda t: (t, 0))
    return pl.pallas_call(
        _merge_kernel,
        grid=(n_rows // rows,),
        in_specs=[row_spec] * 4 + [_const_spec(c.shape) for c in consts],
        out_specs=row_spec,
        out_shape=jax.ShapeDtypeStruct((n_rows, d), F32),
        compiler_params=_params(1),
        name="gated_merge",
    )(h, y_rnn, y_sc, o, *consts)


def _rope_tables(t_all):
    half = QK_ROPE // 2
    inv = ROPE_THETA ** (-jnp.arange(0, QK_ROPE, 2, dtype=F32) / QK_ROPE)
    ang = jnp.arange(t_all, dtype=F32)[:, None] * inv[None, :]
    cos, sin = jnp.cos(ang), jnp.sin(ang)
    zeros = jnp.zeros((t_all, LANES - 2 * half), F32)
    return jnp.tile(cos, (1, LANES // half)), jnp.concatenate([sin, sin, zeros], axis=1)


def _pad_lanes(a, width):
    return jnp.pad(a, [(0, 0)] * (a.ndim - 1) + [(0, width - a.shape[-1])])


def _rotate_half(a):
    half = a.shape[-1] // 2
    return jnp.concatenate([-a[..., half:], a[..., :half]], axis=-1)


def _swap_halves(a):
    half = a.shape[-1] // 2
    return jnp.concatenate([a[..., half:], a[..., :half]], axis=-1)


def kernel(x, meta, norm_g, w_in, rg_conv_w, rg_conv_b, rg_wa, rg_ba, rg_wx, rg_bx, rg_lambda, rg_out,
           sc_conv_w, sc_out, mla_cq_g, mla_w_uq, mla_ckv_g, mla_w_uk, mla_w_uv, mla_qnorm_g,
           mla_knorm_g, mla_out, gate_b, w_out):
    b, seq, d = x.shape
    depth = norm_g.shape[0]
    d_rnn = rg_out.shape[1]
    d_sc = sc_out.shape[1]
    t_all = N_META + seq
    assert b == SUBLANES, "time-major layout needs the batch to fill one f32 sublane tile"
    assert t_all % STEPS_PER_TILE == 0 and t_all % QKV_ROWS == 0

    meta_b = jnp.broadcast_to(meta[:, None].astype(x.dtype), (N_META, b, d))
    h = jnp.concatenate([meta_b, jnp.transpose(x, (1, 0, 2))], axis=0).reshape(t_all * b, d)
    cos_t, sin_t = _rope_tables(t_all)

    o_sc = 2 * d_rnn
    o_cq = o_sc + 4 * d_sc
    o_kr = o_cq + Q_LORA + KV_LORA
    o_ga = o_kr + QK_ROPE
    n_grp = MLA_HEADS // HEAD_GROUP

    row = lambda a: a.reshape(1, -1)
    for l in range(depth):
        wl = w_in[l]
        ng = row(norm_g[l])
        w_rnn = wl[:, :o_sc].astype(BF16)
        w_sc = wl[:, o_sc:o_cq].astype(BF16)
        w_lat = jnp.concatenate([wl[:, o_cq:o_kr], _pad_lanes(wl[:, o_kr:o_ga], LANES)], axis=1).astype(BF16)
        w_mg = wl[:, o_ga:].astype(BF16)

        wq = mla_w_uq[l].reshape(Q_LORA, MLA_HEADS, QK_HEAD)
        wq_rope = wq[:, :, QK_NOPE:]
        wq_heads = jnp.concatenate(
            [wq[:, :, :QK_NOPE], _pad_lanes(wq_rope, LANES), _pad_lanes(_rotate_half(wq_rope), LANES)], axis=2)
        wq_grp = wq_heads.reshape(Q_LORA, n_grp, HEAD_GROUP * 3 * LANES).transpose(1, 0, 2).astype(BF16)
        wk = mla_w_uk[l].reshape(KV_LORA, MLA_HEADS, QK_NOPE)
        wv = mla_w_uv[l].reshape(KV_LORA, MLA_HEADS, V_HEAD)
        wkv_grp = jnp.concatenate([wk, wv], axis=2).reshape(
            KV_LORA, n_grp, HEAD_GROUP * 2 * LANES).transpose(1, 0, 2).astype(BF16)
        qg, kg = mla_qnorm_g[l], mla_knorm_g[l]
        qg_rope = qg[QK_NOPE:]

        y_rnn = _rnn_call(h, ng, w_rnn, rg_conv_w[l], row(rg_conv_b[l]), rg_wa[l].astype(BF16),
                          row(rg_ba[l]), rg_wx[l].astype(BF16), row(rg_bx[l]), row(rg_lambda[l]),
                          rg_out[l].astype(BF16))
        y_sc = _sconv_call(h, ng, w_sc, sc_conv_w[l], sc_out[l].astype(BF16))
        q, k, v = _qkv_call(h.reshape(t_all, b * d), d, ng, w_lat, row(mla_cq_g[l]), row(mla_ckv_g[l]),
                            wq_grp, wkv_grp, row(qg[:QK_NOPE]), _pad_lanes(row(qg_rope), LANES),
                            _pad_lanes(row(_swap_halves(qg_rope)), LANES), row(kg[:QK_NOPE]),
                            _pad_lanes(row(kg[QK_NOPE:]), LANES), cos_t, sin_t)
        o = _attn_call(q, k, v).reshape(t_all * b, MLA_HEADS * V_HEAD)
        h = _merge_call(h, y_rnn, y_sc, o, ng, w_mg, row(gate_b[l]), mla_out[l].astype(BF16),
                        w_out[l].astype(BF16))
    return jnp.transpose(h.reshape(t_all, b, d)[N_META:], (1, 0, 2))
```

```python
import jax
import jax.numpy as jnp
from jax import lax
from jax.experimental import pallas as pl
from jax.experimental.pallas import tpu as pltpu

N_META = 16
EPS = 1e-6
RNN_BLOCKS = 4
RNN_CONV = 4
LRU_C = 8.0
SC_CONV = 3
MLA_HEADS = 8
QK_NOPE = 128
QK_ROPE = 64
QK_HEAD = QK_NOPE + QK_ROPE
V_HEAD = 128
Q_LORA = 384
KV_LORA = 256
ROPE_THETA = 10000.0

LANES = 128
SUBLANES = 8
STEPS_PER_TILE = 48
ATTN_TILE = 256
ATTN_MACRO = 512
HEAD_GROUP = 2
VMEM_LIMIT = 56 * 1024 * 1024

F32 = jnp.float32
BF16 = jnp.bfloat16


def _rms(x, g):
    ms = jnp.mean(x * x, axis=-1, keepdims=True)
    return x * lax.rsqrt(ms + EPS) * g


def _silu(x):
    return x * jax.nn.sigmoid(x)


def _causal_taps(buf, cur, taps, halo, rows):
    width = len(taps)
    acc = taps[width - 1] * cur
    for k in range(width - 1):
        back = (width - 1 - k) * SUBLANES
        acc = acc + taps[k] * buf[pl.ds(halo - back, rows), :]
    return acc


def _rnn_kernel(x_ref, ng_ref, w_ref, cw_ref, cb_ref, wa_ref, ba_ref, wx_ref, bx_ref, lam_ref,
                wo_ref, y_ref, xbuf, h_buf, h_carry):
    rows = x_ref.shape[0]
    d_rnn = wo_ref.shape[0]
    blk = d_rnn // RNN_BLOCKS
    halo = (RNN_CONV - 1) * SUBLANES

    @pl.when(pl.program_id(0) == 0)
    def _():
        xbuf[pl.ds(0, halo), :] = jnp.zeros((halo, d_rnn), F32)
        h_carry[...] = jnp.zeros_like(h_carry)

    h = _rms(x_ref[...], ng_ref[...]).astype(BF16)
    z = jnp.dot(h, w_ref[...], preferred_element_type=F32)
    xr = z[:, :d_rnn]
    gate = z[:, d_rnn:]

    xbuf[pl.ds(halo, rows), :] = xr
    taps = [cw_ref[pl.ds(k, 1), :] for k in range(RNN_CONV)]
    xc = _causal_taps(xbuf, xr, taps, halo, rows) + cb_ref[...]
    xbuf[pl.ds(0, halo), :] = xbuf[pl.ds(rows, halo), :]

    xcb = xc.astype(BF16)
    ra, ri = [], []
    for n in range(RNN_BLOCKS):
        xn = xcb[:, n * blk:(n + 1) * blk]
        ra.append(jnp.dot(xn, wa_ref[n], preferred_element_type=F32))
        ri.append(jnp.dot(xn, wx_ref[n], preferred_element_type=F32))
    r = jax.nn.sigmoid(jnp.concatenate(ra, axis=-1) + ba_ref[...])
    i = jax.nn.sigmoid(jnp.concatenate(ri, axis=-1) + bx_ref[...])

    lam = lam_ref[...]
    softplus_neg = jnp.maximum(-lam, 0.0) + jnp.log1p(jnp.exp(-jnp.abs(lam)))
    a = jnp.exp((-LRU_C) * r * softplus_neg)
    u = jnp.sqrt(1.0 - a * a) * (i * xc)

    hc = h_carry[...]
    for t in range(rows // SUBLANES):
        sl = slice(t * SUBLANES, (t + 1) * SUBLANES)
        hc = a[sl] * hc + u[sl]
        h_buf[sl, :] = hc
    h_carry[...] = hc

    y = (h_buf[...] * _silu(gate)).astype(BF16)
    y_ref[...] = jnp.dot(y, wo_ref[...], preferred_element_type=F32).astype(y_ref.dtype)


def _sconv_kernel(x_ref, ng_ref, w_ref, cw_ref, wo_ref, y_ref, cbuf):
    rows = x_ref.shape[0]
    d_sc = wo_ref.shape[0]
    halo = (SC_CONV - 1) * SUBLANES

    @pl.when(pl.program_id(0) == 0)
    def _():
        cbuf[pl.ds(0, halo), :] = jnp.zeros((halo, d_sc), F32)

    h = _rms(x_ref[...], ng_ref[...]).astype(BF16)
    z = jnp.dot(h, w_ref[...], preferred_element_type=F32)
    sc_b = z[:, :d_sc]
    cx = z[:, d_sc:2 * d_sc] * z[:, 2 * d_sc:3 * d_sc]
    gate = z[:, 3 * d_sc:]

    cbuf[pl.ds(halo, rows), :] = cx
    taps = [cw_ref[pl.ds(k, 1), :] for k in range(SC_CONV)]
    conv = _causal_taps(cbuf, cx, taps, halo, rows)
    cbuf[pl.ds(0, halo), :] = cbuf[pl.ds(rows, halo), :]

    y = (sc_b * conv * _silu(gate)).astype(BF16)
    y_ref[...] = jnp.dot(y, wo_ref[...], preferred_element_type=F32).astype(y_ref.dtype)


def _per_batch_rows(buf, steps):
    return jnp.concatenate(
        [jnp.concatenate([buf[j, pl.ds(b, steps, stride=SUBLANES), :] for j in range(buf.shape[0])], axis=1)
         for b in range(SUBLANES)], axis=0)


def _qkv_kernel(x_ref, ng_ref, w_ref, cqg_ref, ckvg_ref, wq_ref, wkv_ref, qgn_ref, qgr_ref,
                kgn_ref, kgr_ref, tab_ref, q_ref, k_ref, v_ref, lat_buf):
    rows = x_ref.shape[0]
    steps = rows // SUBLANES
    n_lat = Q_LORA + KV_LORA
    scale = QK_HEAD ** -0.5
    inv_head = 1.0 / QK_HEAD

    h = _rms(x_ref[...], ng_ref[...]).astype(BF16)
    z = jnp.dot(h, w_ref[...], preferred_element_type=F32)
    for j in range(lat_buf.shape[0]):
        lat_buf[j] = z[:, j * LANES:(j + 1) * LANES]
    zb = _per_batch_rows(lat_buf, steps)
    kr = zb[:, n_lat:]
    cq = _rms(zb[:, :Q_LORA], cqg_ref[...]).astype(BF16)
    ckv = _rms(zb[:, Q_LORA:n_lat], ckvg_ref[...]).astype(BF16)

    def tiled(j):
        return jnp.concatenate([tab_ref[:, j * LANES:(j + 1) * LANES]] * SUBLANES, axis=0)

    q_tab = tiled(0) * qgr_ref[...]
    kr_half_sq = 0.5 * kr * kr
    krg = kr * kgr_ref[...]
    kr_rot = krg * tiled(1) + pltpu.roll(krg, QK_ROPE // 2, axis=1) * tiled(2)

    def per_batch(a):
        return a.reshape(SUBLANES, steps, a.shape[-1])

    for grp in range(MLA_HEADS // HEAD_GROUP):
        qg = jnp.dot(cq, wq_ref[grp], preferred_element_type=F32)
        kvg = jnp.dot(ckv, wkv_ref[grp], preferred_element_type=F32)
        for j in range(HEAD_GROUP):
            hd = grp * HEAD_GROUP + j
            qn = qg[:, (2 * j) * LANES:(2 * j + 1) * LANES]
            qx = qg[:, (2 * j + 1) * LANES:(2 * j + 2) * LANES]
            q_ss = jnp.sum(qn * qn + 0.5 * (qx * qx), axis=-1, keepdims=True)
            q_inv = lax.rsqrt(q_ss * inv_head + EPS) * scale
            q_ref[:, hd, :, pl.ds(0, QK_NOPE)] = per_batch(qn * q_inv * qgn_ref[...]).astype(q_ref.dtype)
            q_ref[:, hd, :, pl.ds(QK_NOPE, LANES)] = per_batch(qx * q_tab * q_inv).astype(q_ref.dtype)

            kn = kvg[:, (2 * j) * LANES:(2 * j + 1) * LANES]
            k_ss = jnp.sum(kn * kn + kr_half_sq, axis=-1, keepdims=True)
            k_inv = lax.rsqrt(k_ss * inv_head + EPS)
            k_ref[:, hd, :, pl.ds(0, QK_NOPE)] = per_batch(kn * k_inv * kgn_ref[...]).astype(k_ref.dtype)
            k_ref[:, hd, :, pl.ds(QK_NOPE, LANES)] = per_batch(kr_rot * k_inv).astype(k_ref.dtype)
            v_ref[:, hd] = per_batch(kvg[:, (2 * j + 1) * LANES:(2 * j + 2) * LANES]).astype(v_ref.dtype)


def _attn_kernel(q_ref, k_ref, v_ref, o_ref):
    t_all = q_ref.shape[2]
    nt = (((1,), (1,)), ((), ()))

    def keys(lo, n):
        return k_ref[0, 0, pl.ds(lo, n), :]

    def vals(lo, n):
        return v_ref[0, 0, pl.ds(lo, n), :]

    def scores(start, size):
        q = q_ref[0, 0, pl.ds(start, size), :]
        s_off = lax.dot_general(q, keys(0, start), nt, preferred_element_type=F32) if start else None
        sub = min(size, ATTN_TILE)
        row = lax.broadcasted_iota(jnp.int32, (sub, sub), 0)
        col = lax.broadcasted_iota(jnp.int32, (sub, sub), 1)
        near = []
        for r in range(0, size, sub):
            qr = q[r:r + sub]
            s_in = lax.dot_general(qr, keys(start, r), nt, preferred_element_type=F32) if r else None
            s_dg = lax.dot_general(qr, keys(start + r, sub), nt, preferred_element_type=F32)
            near.append((s_in, jnp.where(col <= row, s_dg, -1e30)))
        return s_off, near

    def finish(start, size, s_off, near):
        sub = min(size, ATTN_TILE)
        maxes, p_offs = [], []
        for n, (s_in, s_dg) in enumerate(near):
            m = jnp.max(s_dg, axis=-1, keepdims=True)
            if s_in is not None:
                m = jnp.maximum(m, jnp.max(s_in, axis=-1, keepdims=True))
            if s_off is not None:
                so = s_off[n * sub:(n + 1) * sub]
                m = jnp.maximum(m, jnp.max(so, axis=-1, keepdims=True))
                p_offs.append(jnp.exp(so - m))
            maxes.append(m)
        if s_off is not None:
            p_off = jnp.concatenate(p_offs, axis=0) if len(p_offs) > 1 else p_offs[0]
            acc_off = jnp.dot(p_off.astype(BF16), vals(0, start), preferred_element_type=F32)
        for n, (s_in, s_dg) in enumerate(near):
            r = n * sub
            m = maxes[n]
            p_dg = jnp.exp(s_dg - m)
            l = jnp.sum(p_dg, axis=-1, keepdims=True)
            acc = jnp.dot(p_dg.astype(BF16), vals(start + r, sub), preferred_element_type=F32)
            if s_in is not None:
                p_in = jnp.exp(s_in - m)
                l = l + jnp.sum(p_in, axis=-1, keepdims=True)
                acc = acc + jnp.dot(p_in.astype(BF16), vals(start, r), preferred_element_type=F32)
            if s_off is not None:
                l = l + jnp.sum(p_offs[n], axis=-1, keepdims=True)
                acc = acc + acc_off[r:r + sub]
            o_ref[0, pl.ds(start + r, sub), :] = (acc / l).astype(o_ref.dtype)

    tiles = [(s, ATTN_MACRO) for s in range(0, t_all - ATTN_MACRO + 1, ATTN_MACRO)]
    done = len(tiles) * ATTN_MACRO
    if done < t_all:
        tiles.append((done, t_all - done))
    pending = scores(*tiles[0])
    for idx, (start, size) in enumerate(tiles):
        nxt = scores(*tiles[idx + 1]) if idx + 1 < len(tiles) else None
        finish(start, size, *pending)
        pending = nxt


def _merge_kernel(x_ref, yr_ref, ys_ref, o_ref, ng_ref, w_ref, gb_ref, wao_ref, wout_ref, out_ref, o_stage):
    d = x_ref.shape[1]
    steps = o_ref.shape[1]
    for b in range(SUBLANES):
        ob = o_ref[b].astype(F32)
        for j in range(o_stage.shape[0]):
            o_stage[j, pl.ds(b, steps, stride=SUBLANES), :] = ob[:, j * LANES:(j + 1) * LANES]
    o_tm = jnp.concatenate([o_stage[j] for j in range(o_stage.shape[0])], axis=1)

    x = x_ref[...]
    h = _rms(x, ng_ref[...]).astype(BF16)
    z = jnp.dot(h, w_ref[...], preferred_element_type=F32)
    att = (o_tm * _silu(z[:, :d])).astype(BF16)
    y_att = jnp.dot(att, wao_ref[...], preferred_element_type=F32)
    gates = jax.nn.sigmoid(z[:, d:] + gb_ref[...])
    merged = (gates[:, :d] * yr_ref[...].astype(F32) + gates[:, d:2 * d] * ys_ref[...].astype(F32)
              + gates[:, 2 * d:] * y_att)
    out_ref[...] = x + jnp.dot(merged.astype(BF16), wout_ref[...], preferred_element_type=F32)


def _const_spec(shape):
    nd = len(shape)
    return pl.BlockSpec(shape, lambda *_: (0,) * nd)


def _params(n_axes):
    return pltpu.CompilerParams(dimension_semantics=("arbitrary",) * n_axes, vmem_limit_bytes=VMEM_LIMIT)


def _tile_rows():
    return STEPS_PER_TILE * SUBLANES


def _rnn_call(h, ng, w, cw, cb, wa, ba, wx, bx, lam, wo):
    n_rows, d = h.shape
    d_rnn = wo.shape[0]
    rows = _tile_rows()
    consts = (ng, w, cw, cb, wa, ba, wx, bx, lam, wo)
    row_spec = pl.BlockSpec((rows, d), lambda t: (t, 0))
    return pl.pallas_call(
        _rnn_kernel,
        grid=(n_rows // rows,),
        in_specs=[row_spec] + [_const_spec(c.shape) for c in consts],
        out_specs=row_spec,
        out_shape=jax.ShapeDtypeStruct((n_rows, d), BF16),
        scratch_shapes=[pltpu.VMEM((rows + (RNN_CONV - 1) * SUBLANES, d_rnn), F32),
                        pltpu.VMEM((rows, d_rnn), F32), pltpu.VMEM((SUBLANES, d_rnn), F32)],
        compiler_params=_params(1),
        name="rnn_branch",
    )(h, *consts)


def _sconv_call(h, ng, w, cw, wo):
    n_rows, d = h.shape
    d_sc = wo.shape[0]
    rows = _tile_rows()
    consts = (ng, w, cw, wo)
    row_spec = pl.BlockSpec((rows, d), lambda t: (t, 0))
    return pl.pallas_call(
        _sconv_kernel,
        grid=(n_rows // rows,),
        in_specs=[row_spec] + [_const_spec(c.shape) for c in consts],
        out_specs=row_spec,
        out_shape=jax.ShapeDtypeStruct((n_rows, d), BF16),
        scratch_shapes=[pltpu.VMEM((rows + (SC_CONV - 1) * SUBLANES, d_sc), F32)],
        compiler_params=_params(1),
        name="sconv_branch",
    )(h, *consts)


def _qkv_call(h, ng, w, cqg, ckvg, wq, wkv, qgn, qgr, kgn, kgr, tabs):
    n_rows, d = h.shape
    rows = _tile_rows()
    t_all = n_rows // SUBLANES
    consts = (ng, w, cqg, ckvg, wq, wkv, qgn, qgr, kgn, kgr)
    tab_spec = pl.BlockSpec((STEPS_PER_TILE, tabs.shape[1]), lambda t: (t, 0))
    qk_spec = pl.BlockSpec((SUBLANES, MLA_HEADS, STEPS_PER_TILE, 2 * LANES), lambda t: (0, 0, t, 0))
    v_spec = pl.BlockSpec((SUBLANES, MLA_HEADS, STEPS_PER_TILE, V_HEAD), lambda t: (0, 0, t, 0))
    return pl.pallas_call(
        _qkv_kernel,
        grid=(n_rows // rows,),
        in_specs=[pl.BlockSpec((rows, d), lambda t: (t, 0))] + [_const_spec(c.shape) for c in consts]
        + [tab_spec],
        out_specs=[qk_spec, qk_spec, v_spec],
        out_shape=[jax.ShapeDtypeStruct((SUBLANES, MLA_HEADS, t_all, 2 * LANES), BF16),
                   jax.ShapeDtypeStruct((SUBLANES, MLA_HEADS, t_all, 2 * LANES), BF16),
                   jax.ShapeDtypeStruct((SUBLANES, MLA_HEADS, t_all, V_HEAD), BF16)],
        scratch_shapes=[pltpu.VMEM((w.shape[1] // LANES, rows, LANES), F32)],
        compiler_params=_params(1),
        name="mla_qkv",
    )(h, *consts, tabs)


def _attn_call(q, k, v):
    b, nh, t_all, dk = q.shape
    return pl.pallas_call(
        _attn_kernel,
        grid=(b, nh),
        in_specs=[pl.BlockSpec((1, 1, t_all, dk), lambda bi, hi: (bi, hi, 0, 0)),
                  pl.BlockSpec((1, 1, t_all, dk), lambda bi, hi: (bi, hi, 0, 0)),
                  pl.BlockSpec((1, 1, t_all, V_HEAD), lambda bi, hi: (bi, hi, 0, 0))],
        out_specs=pl.BlockSpec((1, t_all, V_HEAD), lambda bi, hi: (bi, 0, hi)),
        out_shape=jax.ShapeDtypeStruct((b, t_all, nh * V_HEAD), BF16),
        compiler_params=_params(2),
        name="mla_attention",
    )(q, k, v)


def _merge_call(h, y_rnn, y_sc, o, ng, w, gb, wao, wout):
    n_rows, d = h.shape
    rows = _tile_rows()
    consts = (ng, w, gb, wao, wout)
    row_spec = pl.BlockSpec((rows, d), lambda t: (t, 0))
    o_spec = pl.BlockSpec((SUBLANES, STEPS_PER_TILE, d), lambda t: (0, t, 0))
    return pl.pallas_call(
        _merge_kernel,
        grid=(n_rows // rows,),
        in_specs=[row_spec] * 3 + [o_spec] + [_const_spec(c.shape) for c in consts],
        out_specs=row_spec,
        out_shape=jax.ShapeDtypeStruct((n_rows, d), F32),
        scratch_shapes=[pltpu.VMEM((d // LANES, rows, LANES), F32)],
        compiler_params=_params(1),
        name="gated_merge",
    )(h, y_rnn, y_sc, o, *consts)


def _rope_tables(t_all):
    inv = ROPE_THETA ** (-jnp.arange(0, QK_ROPE, 2, dtype=F32) / QK_ROPE)
    ang = jnp.arange(t_all, dtype=F32)[:, None] * inv[None, :]
    c, s = jnp.cos(ang), jnp.sin(ang)
    return jnp.concatenate([c, c, s, s, c, c, c, c, -s, s, -s, s], axis=1)


def _rotate_half(a):
    half = a.shape[-1] // 2
    return jnp.concatenate([-a[..., half:], a[..., :half]], axis=-1)


def _swap_halves(a):
    half = a.shape[-1] // 2
    return jnp.concatenate([a[..., half:], a[..., :half]], axis=-1)


def kernel(x, meta, norm_g, w_in, rg_conv_w, rg_conv_b, rg_wa, rg_ba, rg_wx, rg_bx, rg_lambda, rg_out,
           sc_conv_w, sc_out, mla_cq_g, mla_w_uq, mla_ckv_g, mla_w_uk, mla_w_uv, mla_qnorm_g,
           mla_knorm_g, mla_out, gate_b, w_out):
    b, seq, d = x.shape
    depth = norm_g.shape[0]
    d_rnn = rg_out.shape[1]
    d_sc = sc_out.shape[1]
    t_all = N_META + seq
    assert b == SUBLANES, "time-major layout needs the batch to fill one f32 sublane tile"
    assert t_all % STEPS_PER_TILE == 0

    meta_b = jnp.broadcast_to(meta[:, None].astype(x.dtype), (N_META, b, d))
    h = jnp.concatenate([meta_b, jnp.transpose(x, (1, 0, 2))], axis=0).reshape(t_all * b, d)
    tabs = _rope_tables(t_all)

    o_sc = 2 * d_rnn
    o_cq = o_sc + 4 * d_sc
    o_kr = o_cq + Q_LORA + KV_LORA
    o_ga = o_kr + QK_ROPE
    n_grp = MLA_HEADS // HEAD_GROUP

    row = lambda a: a.reshape(1, -1)
    for l in range(depth):
        wl = w_in[l]
        ng = row(norm_g[l])
        w_rnn = wl[:, :o_sc].astype(BF16)
        w_sc = wl[:, o_sc:o_cq].astype(BF16)
        w_kr = wl[:, o_kr:o_ga]
        w_lat = jnp.concatenate([wl[:, o_cq:o_kr], w_kr, w_kr], axis=1).astype(BF16)
        w_mg = wl[:, o_ga:].astype(BF16)

        wq = mla_w_uq[l].reshape(Q_LORA, MLA_HEADS, QK_HEAD)
        wq_rope = wq[:, :, QK_NOPE:]
        wq_heads = jnp.concatenate([wq[:, :, :QK_NOPE], wq_rope, _rotate_half(wq_rope)], axis=2)
        wq_grp = wq_heads.reshape(Q_LORA, n_grp, HEAD_GROUP * 2 * LANES).transpose(1, 0, 2).astype(BF16)
        wk = mla_w_uk[l].reshape(KV_LORA, MLA_HEADS, QK_NOPE)
        wv = mla_w_uv[l].reshape(KV_LORA, MLA_HEADS, V_HEAD)
        wkv_grp = jnp.concatenate([wk, wv], axis=2).reshape(
            KV_LORA, n_grp, HEAD_GROUP * 2 * LANES).transpose(1, 0, 2).astype(BF16)
        qg, kg = mla_qnorm_g[l], mla_knorm_g[l]
        qg_rope, kg_rope = qg[QK_NOPE:], kg[QK_NOPE:]

        y_rnn = _rnn_call(h, ng, w_rnn, rg_conv_w[l], row(rg_conv_b[l]), rg_wa[l].astype(BF16),
                          row(rg_ba[l]), rg_wx[l].astype(BF16), row(rg_bx[l]), row(rg_lambda[l]),
                          rg_out[l].astype(BF16))
        y_sc = _sconv_call(h, ng, w_sc, sc_conv_w[l], sc_out[l].astype(BF16))
        q, k, v = _qkv_call(h, ng, w_lat, row(mla_cq_g[l]), row(mla_ckv_g[l]), wq_grp, wkv_grp,
                            row(qg[:QK_NOPE]), row(jnp.concatenate([qg_rope, _swap_halves(qg_rope)])),
                            row(kg[:QK_NOPE]), row(jnp.concatenate([kg_rope, kg_rope])), tabs)
        o = _attn_call(q, k, v)
        h = _merge_call(h, y_rnn, y_sc, o, ng, w_mg, row(gate_b[l]), mla_out[l].astype(BF16),
                        w_out[l].astype(BF16))
    return jnp.transpose(h.reshape(t_all, b, d)[N_META:], (1, 0, 2))
```

```python
import jax
import jax.numpy as jnp
from jax import lax
from jax.experimental import pallas as pl
from jax.experimental.pallas import tpu as pltpu

N_META = 16
EPS = 1e-6
RNN_BLOCKS = 4
RNN_CONV = 4
LRU_C = 8.0
SC_CONV = 3
MLA_HEADS = 8
QK_NOPE = 128
QK_ROPE = 64
QK_HEAD = QK_NOPE + QK_ROPE
V_HEAD = 128
Q_LORA = 384
KV_LORA = 256
ROPE_THETA = 10000.0

LANES = 128
SUBLANES = 8
STEPS_PER_TILE = 48
ATTN_TILE = 256
ATTN_MACRO = 512
HEAD_GROUP = 2
W_BLOCK = 2048
VMEM_LIMIT = 56 * 1024 * 1024
MIXER_VMEM_LIMIT = 60 * 1024 * 1024

F32 = jnp.float32
BF16 = jnp.bfloat16


def _rms(x, g):
    ms = jnp.mean(x * x, axis=-1, keepdims=True)
    return x * lax.rsqrt(ms + EPS) * g


def _silu(x):
    return x * jax.nn.sigmoid(x)


def _causal_taps(buf, cur, taps, halo, rows):
    width = len(taps)
    acc = taps[width - 1] * cur
    for k in range(width - 1):
        back = (width - 1 - k) * SUBLANES
        acc = acc + taps[k] * buf[pl.ds(halo - back, rows), :]
    return acc


def _mixer_kernel(x_ref, o_ref, ng_ref, wr_ref, ws0_ref, ws1_ref, wm_ref,
                  rcw_ref, rcb_ref, wa_ref, ba_ref, wx_ref, bx_ref, lam_ref, wro_ref,
                  scw_ref, wso_ref, gb_ref, wao_ref, wout_ref,
                  out_ref, xbuf, h_buf, h_carry, cbuf, o_stage):
    rows, d = x_ref.shape
    steps = o_ref.shape[1]
    d_rnn = wro_ref.shape[0]
    d_sc = wso_ref.shape[0]
    blk = d_rnn // RNN_BLOCKS
    rnn_halo = (RNN_CONV - 1) * SUBLANES
    sc_halo = (SC_CONV - 1) * SUBLANES

    @pl.when(pl.program_id(0) == 0)
    def _():
        xbuf[pl.ds(0, rnn_halo), :] = jnp.zeros((rnn_halo, d_rnn), F32)
        cbuf[pl.ds(0, sc_halo), :] = jnp.zeros((sc_halo, d_sc), F32)
        h_carry[...] = jnp.zeros_like(h_carry)

    x = x_ref[...]
    h = _rms(x, ng_ref[...]).astype(BF16)

    zr = jnp.dot(h, wr_ref[...], preferred_element_type=F32)
    xr = zr[:, :d_rnn]
    xbuf[pl.ds(rnn_halo, rows), :] = xr
    taps = [rcw_ref[pl.ds(k, 1), :] for k in range(RNN_CONV)]
    xc = _causal_taps(xbuf, xr, taps, rnn_halo, rows) + rcb_ref[...]
    xbuf[pl.ds(0, rnn_halo), :] = xbuf[pl.ds(rows, rnn_halo), :]
    xcb = xc.astype(BF16)
    ra, ri = [], []
    for n in range(RNN_BLOCKS):
        xn = xcb[:, n * blk:(n + 1) * blk]
        ra.append(jnp.dot(xn, wa_ref[n], preferred_element_type=F32))
        ri.append(jnp.dot(xn, wx_ref[n], preferred_element_type=F32))

    zs0 = jnp.dot(h, ws0_ref[...], preferred_element_type=F32)
    zs1 = jnp.dot(h, ws1_ref[...], preferred_element_type=F32)

    r = jax.nn.sigmoid(jnp.concatenate(ra, axis=-1) + ba_ref[...])
    i = jax.nn.sigmoid(jnp.concatenate(ri, axis=-1) + bx_ref[...])
    lam = lam_ref[...]
    softplus_neg = jnp.maximum(-lam, 0.0) + jnp.log1p(jnp.exp(-jnp.abs(lam)))
    a = jnp.exp((-LRU_C) * r * softplus_neg)
    u = jnp.sqrt(1.0 - a * a) * (i * xc)

    zm = jnp.dot(h, wm_ref[...], preferred_element_type=F32)

    hc = h_carry[...]
    for t in range(rows // SUBLANES):
        sl = slice(t * SUBLANES, (t + 1) * SUBLANES)
        hc = a[sl] * hc + u[sl]
        h_buf[sl, :] = hc
    h_carry[...] = hc

    cx = zs0[:, d_sc:] * zs1[:, :d_sc]
    cbuf[pl.ds(sc_halo, rows), :] = cx
    staps = [scw_ref[pl.ds(k, 1), :] for k in range(SC_CONV)]
    conv = _causal_taps(cbuf, cx, staps, sc_halo, rows)
    cbuf[pl.ds(0, sc_halo), :] = cbuf[pl.ds(rows, sc_halo), :]
    y_sc = jnp.dot((zs0[:, :d_sc] * conv * _silu(zs1[:, d_sc:])).astype(BF16), wso_ref[...],
                   preferred_element_type=F32)

    y_rnn = jnp.dot((h_buf[...] * _silu(zr[:, d_rnn:])).astype(BF16), wro_ref[...],
                    preferred_element_type=F32)

    for b in range(SUBLANES):
        ob = o_ref[b].astype(F32)
        for j in range(o_stage.shape[0]):
            o_stage[j, pl.ds(b, steps, stride=SUBLANES), :] = ob[:, j * LANES:(j + 1) * LANES]
    o_tm = jnp.concatenate([o_stage[j] for j in range(o_stage.shape[0])], axis=1)
    y_att = jnp.dot((o_tm * _silu(zm[:, :d])).astype(BF16), wao_ref[...], preferred_element_type=F32)

    gates = jax.nn.sigmoid(zm[:, d:] + gb_ref[...])
    merged = gates[:, :d] * y_rnn + gates[:, d:2 * d] * y_sc + gates[:, 2 * d:] * y_att
    out_ref[...] = x + jnp.dot(merged.astype(BF16), wout_ref[...], preferred_element_type=F32)


def _per_batch_rows(buf, steps):
    return jnp.concatenate(
        [jnp.concatenate([buf[j, pl.ds(b, steps, stride=SUBLANES), :] for j in range(buf.shape[0])], axis=1)
         for b in range(SUBLANES)], axis=0)


def _qkv_kernel(x_ref, ng_ref, w_ref, cqg_ref, ckvg_ref, wq_ref, wkv_ref, qgn_ref, qgr_ref,
                kgn_ref, kgr_ref, tab_ref, q_ref, k_ref, v_ref, lat_buf):
    rows = x_ref.shape[0]
    steps = rows // SUBLANES
    n_lat = Q_LORA + KV_LORA
    scale = QK_HEAD ** -0.5
    inv_head = 1.0 / QK_HEAD

    h = _rms(x_ref[...], ng_ref[...]).astype(BF16)
    z = jnp.dot(h, w_ref[...], preferred_element_type=F32)
    for j in range(lat_buf.shape[0]):
        lat_buf[j] = z[:, j * LANES:(j + 1) * LANES]
    zb = _per_batch_rows(lat_buf, steps)
    kr = zb[:, n_lat:]
    cq = _rms(zb[:, :Q_LORA], cqg_ref[...]).astype(BF16)
    ckv = _rms(zb[:, Q_LORA:n_lat], ckvg_ref[...]).astype(BF16)

    def tiled(j):
        return jnp.concatenate([tab_ref[:, j * LANES:(j + 1) * LANES]] * SUBLANES, axis=0)

    q_tab = tiled(0) * qgr_ref[...]
    kr_half_sq = 0.5 * kr * kr
    krg = kr * kgr_ref[...]
    kr_rot = krg * tiled(1) + pltpu.roll(krg, QK_ROPE // 2, axis=1) * tiled(2)

    def per_batch(a):
        return a.reshape(SUBLANES, steps, a.shape[-1])

    for grp in range(MLA_HEADS // HEAD_GROUP):
        qg = jnp.dot(cq, wq_ref[grp], preferred_element_type=F32)
        kvg = jnp.dot(ckv, wkv_ref[grp], preferred_element_type=F32)
        for j in range(HEAD_GROUP):
            hd = grp * HEAD_GROUP + j
            qn = qg[:, (2 * j) * LANES:(2 * j + 1) * LANES]
            qx = qg[:, (2 * j + 1) * LANES:(2 * j + 2) * LANES]
            q_ss = jnp.sum(qn * qn + 0.5 * (qx * qx), axis=-1, keepdims=True)
            q_inv = lax.rsqrt(q_ss * inv_head + EPS) * scale
            q_ref[:, hd, :, pl.ds(0, QK_NOPE)] = per_batch(qn * q_inv * qgn_ref[...]).astype(q_ref.dtype)
            q_ref[:, hd, :, pl.ds(QK_NOPE, LANES)] = per_batch(qx * q_tab * q_inv).astype(q_ref.dtype)

            kn = kvg[:, (2 * j) * LANES:(2 * j + 1) * LANES]
            k_ss = jnp.sum(kn * kn + kr_half_sq, axis=-1, keepdims=True)
            k_inv = lax.rsqrt(k_ss * inv_head + EPS)
            k_ref[:, hd, :, pl.ds(0, QK_NOPE)] = per_batch(kn * k_inv * kgn_ref[...]).astype(k_ref.dtype)
            k_ref[:, hd, :, pl.ds(QK_NOPE, LANES)] = per_batch(kr_rot * k_inv).astype(k_ref.dtype)
            v_ref[:, hd] = per_batch(kvg[:, (2 * j + 1) * LANES:(2 * j + 2) * LANES]).astype(v_ref.dtype)


def _attn_kernel(q_ref, k_ref, v_ref, o_ref):
    t_all = q_ref.shape[2]
    nt = (((1,), (1,)), ((), ()))

    def keys(lo, n):
        return k_ref[0, 0, pl.ds(lo, n), :]

    def vals(lo, n):
        return v_ref[0, 0, pl.ds(lo, n), :]

    def scores(start, size):
        q = q_ref[0, 0, pl.ds(start, size), :]
        s_off = lax.dot_general(q, keys(0, start), nt, preferred_element_type=F32) if start else None
        sub = min(size, ATTN_TILE)
        row = lax.broadcasted_iota(jnp.int32, (sub, sub), 0)
        col = lax.broadcasted_iota(jnp.int32, (sub, sub), 1)
        near = []
        for r in range(0, size, sub):
            qr = q[r:r + sub]
            s_in = lax.dot_general(qr, keys(start, r), nt, preferred_element_type=F32) if r else None
            s_dg = lax.dot_general(qr, keys(start + r, sub), nt, preferred_element_type=F32)
            near.append((s_in, jnp.where(col <= row, s_dg, -1e30)))
        return s_off, near

    def finish(start, size, s_off, near):
        sub = min(size, ATTN_TILE)
        maxes, p_offs = [], []
        for n, (s_in, s_dg) in enumerate(near):
            m = jnp.max(s_dg, axis=-1, keepdims=True)
            if s_in is not None:
                m = jnp.maximum(m, jnp.max(s_in, axis=-1, keepdims=True))
            if s_off is not None:
                so = s_off[n * sub:(n + 1) * sub]
                m = jnp.maximum(m, jnp.max(so, axis=-1, keepdims=True))
                p_offs.append(jnp.exp(so - m))
            maxes.append(m)
        if s_off is not None:
            p_off = jnp.concatenate(p_offs, axis=0) if len(p_offs) > 1 else p_offs[0]
            acc_off = jnp.dot(p_off.astype(BF16), vals(0, start), preferred_element_type=F32)
        for n, (s_in, s_dg) in enumerate(near):
            r = n * sub
            m = maxes[n]
            p_dg = jnp.exp(s_dg - m)
            l = jnp.sum(p_dg, axis=-1, keepdims=True)
            acc = jnp.dot(p_dg.astype(BF16), vals(start + r, sub), preferred_element_type=F32)
            if s_in is not None:
                p_in = jnp.exp(s_in - m)
                l = l + jnp.sum(p_in, axis=-1, keepdims=True)
                acc = acc + jnp.dot(p_in.astype(BF16), vals(start, r), preferred_element_type=F32)
            if s_off is not None:
                l = l + jnp.sum(p_offs[n], axis=-1, keepdims=True)
                acc = acc + acc_off[r:r + sub]
            o_ref[0, pl.ds(start + r, sub), :] = (acc / l).astype(o_ref.dtype)

    tiles = [(s, ATTN_MACRO) for s in range(0, t_all - ATTN_MACRO + 1, ATTN_MACRO)]
    done = len(tiles) * ATTN_MACRO
    if done < t_all:
        tiles.append((done, t_all - done))
    pending = scores(*tiles[0])
    for idx, (start, size) in enumerate(tiles):
        nxt = scores(*tiles[idx + 1]) if idx + 1 < len(tiles) else None
        finish(start, size, *pending)
        pending = nxt


def _const_spec(shape):
    nd = len(shape)
    return pl.BlockSpec(shape, lambda *_: (0,) * nd)


def _params(n_axes, vmem_limit=VMEM_LIMIT):
    return pltpu.CompilerParams(dimension_semantics=("arbitrary",) * n_axes, vmem_limit_bytes=vmem_limit)


def _tile_rows():
    return STEPS_PER_TILE * SUBLANES


def _mixer_call(h, o, layer, w_all, ng, wm, rcw, rcb, wa, ba, wx, bx, lam, wro, scw, wso, gb, wao, wout):
    n_rows, d = h.shape
    d_rnn = wro.shape[0]
    d_sc = wso.shape[0]
    assert 2 * d_rnn == W_BLOCK and 2 * d_sc == W_BLOCK, "rnn | conv column groups must be whole W_BLOCKs"
    rows = _tile_rows()
    row_spec = pl.BlockSpec((rows, d), lambda t: (t, 0))
    o_spec = pl.BlockSpec((SUBLANES, STEPS_PER_TILE, d), lambda t: (0, t, 0))

    def w_col_block(j):
        return pl.BlockSpec((None, w_all.shape[1], W_BLOCK), lambda t: (layer, 0, j))

    tail = (rcw, rcb, wa, ba, wx, bx, lam, wro, scw, wso, gb, wao, wout)
    return pl.pallas_call(
        _mixer_kernel,
        grid=(n_rows // rows,),
        in_specs=[row_spec, o_spec, _const_spec(ng.shape), w_col_block(0), w_col_block(1), w_col_block(2),
                  _const_spec(wm.shape)] + [_const_spec(c.shape) for c in tail],
        out_specs=row_spec,
        out_shape=jax.ShapeDtypeStruct((n_rows, d), F32),
        scratch_shapes=[pltpu.VMEM((rows + (RNN_CONV - 1) * SUBLANES, d_rnn), F32),
                        pltpu.VMEM((rows, d_rnn), F32), pltpu.VMEM((SUBLANES, d_rnn), F32),
                        pltpu.VMEM((rows + (SC_CONV - 1) * SUBLANES, d_sc), F32),
                        pltpu.VMEM((d // LANES, rows, LANES), F32)],
        compiler_params=_params(1, MIXER_VMEM_LIMIT),
        name="mixers_merge",
    )(h, o, ng, w_all, w_all, w_all, wm, *tail)


def _qkv_call(h, ng, w, cqg, ckvg, wq, wkv, qgn, qgr, kgn, kgr, tabs):
    n_rows, d = h.shape
    rows = _tile_rows()
    t_all = n_rows // SUBLANES
    consts = (ng, w, cqg, ckvg, wq, wkv, qgn, qgr, kgn, kgr)
    tab_spec = pl.BlockSpec((STEPS_PER_TILE, tabs.shape[1]), lambda t: (t, 0))
    qk_spec = pl.BlockSpec((SUBLANES, MLA_HEADS, STEPS_PER_TILE, 2 * LANES), lambda t: (0, 0, t, 0))
    v_spec = pl.BlockSpec((SUBLANES, MLA_HEADS, STEPS_PER_TILE, V_HEAD), lambda t: (0, 0, t, 0))
    return pl.pallas_call(
        _qkv_kernel,
        grid=(n_rows // rows,),
        in_specs=[pl.BlockSpec((rows, d), lambda t: (t, 0))] + [_const_spec(c.shape) for c in consts]
        + [tab_spec],
        out_specs=[qk_spec, qk_spec, v_spec],
        out_shape=[jax.ShapeDtypeStruct((SUBLANES, MLA_HEADS, t_all, 2 * LANES), BF16),
                   jax.ShapeDtypeStruct((SUBLANES, MLA_HEADS, t_all, 2 * LANES), BF16),
                   jax.ShapeDtypeStruct((SUBLANES, MLA_HEADS, t_all, V_HEAD), BF16)],
        scratch_shapes=[pltpu.VMEM((w.shape[1] // LANES, rows, LANES), F32)],
        compiler_params=_params(1),
        name="mla_qkv",
    )(h, *consts, tabs)


def _attn_call(q, k, v):
    b, nh, t_all, dk = q.shape
    return pl.pallas_call(
        _attn_kernel,
        grid=(b, nh),
        in_specs=[pl.BlockSpec((1, 1, t_all, dk), lambda bi, hi: (bi, hi, 0, 0)),
                  pl.BlockSpec((1, 1, t_all, dk), lambda bi, hi: (bi, hi, 0, 0)),
                  pl.BlockSpec((1, 1, t_all, V_HEAD), lambda bi, hi: (bi, hi, 0, 0))],
        out_specs=pl.BlockSpec((1, t_all, V_HEAD), lambda bi, hi: (bi, 0, hi)),
        out_shape=jax.ShapeDtypeStruct((b, t_all, nh * V_HEAD), BF16),
        compiler_params=_params(2),
        name="mla_attention",
    )(q, k, v)


def _rope_tables(t_all):
    inv = ROPE_THETA ** (-jnp.arange(0, QK_ROPE, 2, dtype=F32) / QK_ROPE)
    ang = jnp.arange(t_all, dtype=F32)[:, None] * inv[None, :]
    c, s = jnp.cos(ang), jnp.sin(ang)
    return jnp.concatenate([c, c, s, s, c, c, c, c, -s, s, -s, s], axis=1)


def _rotate_half(a):
    half = a.shape[-1] // 2
    return jnp.concatenate([-a[..., half:], a[..., :half]], axis=-1)


def _swap_halves(a):
    half = a.shape[-1] // 2
    return jnp.concatenate([a[..., half:], a[..., :half]], axis=-1)


def kernel(x, meta, norm_g, w_in, rg_conv_w, rg_conv_b, rg_wa, rg_ba, rg_wx, rg_bx, rg_lambda, rg_out,
           sc_conv_w, sc_out, mla_cq_g, mla_w_uq, mla_ckv_g, mla_w_uk, mla_w_uv, mla_qnorm_g,
           mla_knorm_g, mla_out, gate_b, w_out):
    b, seq, d = x.shape
    depth = norm_g.shape[0]
    d_rnn = rg_out.shape[1]
    d_sc = sc_out.shape[1]
    t_all = N_META + seq
    assert b == SUBLANES, "time-major layout needs the batch to fill one f32 sublane tile"
    assert t_all % STEPS_PER_TILE == 0

    meta_b = jnp.broadcast_to(meta[:, None].astype(x.dtype), (N_META, b, d))
    h = jnp.concatenate([meta_b, jnp.transpose(x, (1, 0, 2))], axis=0).reshape(t_all * b, d)
    tabs = _rope_tables(t_all)

    o_sc = 2 * d_rnn
    o_cq = o_sc + 4 * d_sc
    o_kr = o_cq + Q_LORA + KV_LORA
    o_ga = o_kr + QK_ROPE
    n_grp = MLA_HEADS // HEAD_GROUP
    w_all = w_in.astype(BF16)

    row = lambda a: a.reshape(1, -1)
    for l in range(depth):
        wl = w_all[l]
        ng = row(norm_g[l])
        w_kr = wl[:, o_kr:o_ga]
        w_lat = jnp.concatenate([wl[:, o_cq:o_kr], w_kr, w_kr], axis=1)
        w_mg = wl[:, o_ga:]

        wq = mla_w_uq[l].reshape(Q_LORA, MLA_HEADS, QK_HEAD)
        wq_rope = wq[:, :, QK_NOPE:]
        wq_heads = jnp.concatenate([wq[:, :, :QK_NOPE], wq_rope, _rotate_half(wq_rope)], axis=2)
        wq_grp = wq_heads.reshape(Q_LORA, n_grp, HEAD_GROUP * 2 * LANES).transpose(1, 0, 2).astype(BF16)
        wk = mla_w_uk[l].reshape(KV_LORA, MLA_HEADS, QK_NOPE)
        wv = mla_w_uv[l].reshape(KV_LORA, MLA_HEADS, V_HEAD)
        wkv_grp = jnp.concatenate([wk, wv], axis=2).reshape(
            KV_LORA, n_grp, HEAD_GROUP * 2 * LANES).transpose(1, 0, 2).astype(BF16)
        qg, kg = mla_qnorm_g[l], mla_knorm_g[l]
        qg_rope, kg_rope = qg[QK_NOPE:], kg[QK_NOPE:]

        q, k, v = _qkv_call(h, ng, w_lat, row(mla_cq_g[l]), row(mla_ckv_g[l]), wq_grp, wkv_grp,
                            row(qg[:QK_NOPE]), row(jnp.concatenate([qg_rope, _swap_halves(qg_rope)])),
                            row(kg[:QK_NOPE]), row(jnp.concatenate([kg_rope, kg_rope])), tabs)
        o = _attn_call(q, k, v)
        h = _mixer_call(h, o, l, w_all, ng, w_mg, rg_conv_w[l], row(rg_conv_b[l]), rg_wa[l].astype(BF16),
                        row(rg_ba[l]), rg_wx[l].astype(BF16), row(rg_bx[l]), row(rg_lambda[l]),
                        rg_out[l].astype(BF16), sc_conv_w[l], sc_out[l].astype(BF16), row(gate_b[l]),
                        mla_out[l].astype(BF16), w_out[l].astype(BF16))
    return jnp.transpose(h.reshape(t_all, b, d)[N_META:], (1, 0, 2))
```

```python
import jax
import jax.numpy as jnp
from jax import lax
from jax.experimental import pallas as pl
from jax.experimental.pallas import tpu as pltpu

N_META = 16
EPS = 1e-6
RNN_BLOCKS = 4
RNN_CONV = 4
LRU_C = 8.0
SC_CONV = 3
MLA_HEADS = 8
QK_NOPE = 128
QK_ROPE = 64
QK_HEAD = QK_NOPE + QK_ROPE
V_HEAD = 128
Q_LORA = 384
KV_LORA = 256
ROPE_THETA = 10000.0

LANES = 128
SUBLANES = 8
STEPS_PER_TILE = 48
ATTN_TILE = 256
ATTN_MACRO = 512
HEAD_GROUP = 2
W_BLOCK = 2048
PACK_STEPS = 16
UNPACK_BLOCKS = 4
VMEM_LIMIT = 56 * 1024 * 1024
MIXER_VMEM_LIMIT = 60 * 1024 * 1024

F32 = jnp.float32
BF16 = jnp.bfloat16


def _rms(x, g):
    ms = jnp.mean(x * x, axis=-1, keepdims=True)
    return x * lax.rsqrt(ms + EPS) * g


def _silu(x):
    return x * jax.nn.sigmoid(x)


def _causal_taps(buf, cur, taps, halo, rows):
    width = len(taps)
    acc = taps[width - 1] * cur
    for k in range(width - 1):
        back = (width - 1 - k) * SUBLANES
        acc = acc + taps[k] * buf[pl.ds(halo - back, rows), :]
    return acc


def _mixer_kernel(x_ref, o_ref, ng_ref, wr_ref, ws0_ref, ws1_ref, wm_ref,
                  rcw_ref, rcb_ref, wa_ref, ba_ref, wx_ref, bx_ref, lam_ref, wro_ref,
                  scw_ref, wso_ref, gb_ref, wao_ref, wout_ref,
                  out_ref, xbuf, h_buf, h_carry, cbuf, o_stage):
    rows, d = x_ref.shape
    steps = o_ref.shape[1]
    d_rnn = wro_ref.shape[0]
    d_sc = wso_ref.shape[0]
    blk = d_rnn // RNN_BLOCKS
    rnn_halo = (RNN_CONV - 1) * SUBLANES
    sc_halo = (SC_CONV - 1) * SUBLANES

    @pl.when(pl.program_id(0) == 0)
    def _():
        xbuf[pl.ds(0, rnn_halo), :] = jnp.zeros((rnn_halo, d_rnn), F32)
        cbuf[pl.ds(0, sc_halo), :] = jnp.zeros((sc_halo, d_sc), F32)
        h_carry[...] = jnp.zeros_like(h_carry)

    x = x_ref[...]
    h = _rms(x, ng_ref[...]).astype(BF16)

    zr = jnp.dot(h, wr_ref[...], preferred_element_type=F32)
    xr = zr[:, :d_rnn]
    xbuf[pl.ds(rnn_halo, rows), :] = xr
    taps = [rcw_ref[pl.ds(k, 1), :] for k in range(RNN_CONV)]
    xc = _causal_taps(xbuf, xr, taps, rnn_halo, rows) + rcb_ref[...]
    xbuf[pl.ds(0, rnn_halo), :] = xbuf[pl.ds(rows, rnn_halo), :]
    xcb = xc.astype(BF16)
    ra, ri = [], []
    for n in range(RNN_BLOCKS):
        xn = xcb[:, n * blk:(n + 1) * blk]
        ra.append(jnp.dot(xn, wa_ref[n], preferred_element_type=F32))
        ri.append(jnp.dot(xn, wx_ref[n], preferred_element_type=F32))

    zs0 = jnp.dot(h, ws0_ref[...], preferred_element_type=F32)
    zs1 = jnp.dot(h, ws1_ref[...], preferred_element_type=F32)

    r = jax.nn.sigmoid(jnp.concatenate(ra, axis=-1) + ba_ref[...])
    i = jax.nn.sigmoid(jnp.concatenate(ri, axis=-1) + bx_ref[...])
    lam = lam_ref[...]
    softplus_neg = jnp.maximum(-lam, 0.0) + jnp.log1p(jnp.exp(-jnp.abs(lam)))
    a = jnp.exp((-LRU_C) * r * softplus_neg)
    u = jnp.sqrt(1.0 - a * a) * (i * xc)

    zm = jnp.dot(h, wm_ref[...], preferred_element_type=F32)

    hc = h_carry[...]
    for t in range(rows // SUBLANES):
        sl = slice(t * SUBLANES, (t + 1) * SUBLANES)
        hc = a[sl] * hc + u[sl]
        h_buf[sl, :] = hc
    h_carry[...] = hc

    cx = zs0[:, d_sc:] * zs1[:, :d_sc]
    cbuf[pl.ds(sc_halo, rows), :] = cx
    staps = [scw_ref[pl.ds(k, 1), :] for k in range(SC_CONV)]
    conv = _causal_taps(cbuf, cx, staps, sc_halo, rows)
    cbuf[pl.ds(0, sc_halo), :] = cbuf[pl.ds(rows, sc_halo), :]
    y_sc = jnp.dot((zs0[:, :d_sc] * conv * _silu(zs1[:, d_sc:])).astype(BF16), wso_ref[...],
                   preferred_element_type=F32)

    y_rnn = jnp.dot((h_buf[...] * _silu(zr[:, d_rnn:])).astype(BF16), wro_ref[...],
                    preferred_element_type=F32)

    for b in range(SUBLANES):
        ob = o_ref[b].astype(F32)
        for j in range(o_stage.shape[0]):
            o_stage[j, pl.ds(b, steps, stride=SUBLANES), :] = ob[:, j * LANES:(j + 1) * LANES]
    o_tm = jnp.concatenate([o_stage[j] for j in range(o_stage.shape[0])], axis=1)
    y_att = jnp.dot((o_tm * _silu(zm[:, :d])).astype(BF16), wao_ref[...], preferred_element_type=F32)

    gates = jax.nn.sigmoid(zm[:, d:] + gb_ref[...])
    merged = gates[:, :d] * y_rnn + gates[:, d:2 * d] * y_sc + gates[:, 2 * d:] * y_att
    out_ref[...] = x + jnp.dot(merged.astype(BF16), wout_ref[...], preferred_element_type=F32)


def _per_batch_rows(buf, steps):
    return jnp.concatenate(
        [jnp.concatenate([buf[j, pl.ds(b, steps, stride=SUBLANES), :] for j in range(buf.shape[0])], axis=1)
         for b in range(SUBLANES)], axis=0)


def _qkv_kernel(x_ref, ng_ref, w_ref, cqg_ref, ckvg_ref, wq_ref, wkv_ref, qgn_ref, qgr_ref,
                kgn_ref, kgr_ref, tab_ref, q_ref, k_ref, v_ref, lat_buf):
    rows = x_ref.shape[0]
    steps = rows // SUBLANES
    n_lat = Q_LORA + KV_LORA
    scale = QK_HEAD ** -0.5
    inv_head = 1.0 / QK_HEAD

    h = _rms(x_ref[...], ng_ref[...]).astype(BF16)
    z = jnp.dot(h, w_ref[...], preferred_element_type=F32)
    for j in range(lat_buf.shape[0]):
        lat_buf[j] = z[:, j * LANES:(j + 1) * LANES]
    zb = _per_batch_rows(lat_buf, steps)
    kr = zb[:, n_lat:]
    cq = _rms(zb[:, :Q_LORA], cqg_ref[...]).astype(BF16)
    ckv = _rms(zb[:, Q_LORA:n_lat], ckvg_ref[...]).astype(BF16)

    def tiled(j):
        return jnp.concatenate([tab_ref[:, j * LANES:(j + 1) * LANES]] * SUBLANES, axis=0)

    q_tab = tiled(0) * qgr_ref[...]
    kr_half_sq = 0.5 * kr * kr
    krg = kr * kgr_ref[...]
    kr_rot = krg * tiled(1) + pltpu.roll(krg, QK_ROPE // 2, axis=1) * tiled(2)

    def per_batch(a):
        return a.reshape(SUBLANES, steps, a.shape[-1])

    for grp in range(MLA_HEADS // HEAD_GROUP):
        qg = jnp.dot(cq, wq_ref[grp], preferred_element_type=F32)
        kvg = jnp.dot(ckv, wkv_ref[grp], preferred_element_type=F32)
        for j in range(HEAD_GROUP):
            hd = grp * HEAD_GROUP + j
            qn = qg[:, (2 * j) * LANES:(2 * j + 1) * LANES]
            qx = qg[:, (2 * j + 1) * LANES:(2 * j + 2) * LANES]
            q_ss = jnp.sum(qn * qn + 0.5 * (qx * qx), axis=-1, keepdims=True)
            q_inv = lax.rsqrt(q_ss * inv_head + EPS) * scale
            q_ref[:, hd, :, pl.ds(0, QK_NOPE)] = per_batch(qn * q_inv * qgn_ref[...]).astype(q_ref.dtype)
            q_ref[:, hd, :, pl.ds(QK_NOPE, LANES)] = per_batch(qx * q_tab * q_inv).astype(q_ref.dtype)

            kn = kvg[:, (2 * j) * LANES:(2 * j + 1) * LANES]
            k_ss = jnp.sum(kn * kn + kr_half_sq, axis=-1, keepdims=True)
            k_inv = lax.rsqrt(k_ss * inv_head + EPS)
            k_ref[:, hd, :, pl.ds(0, QK_NOPE)] = per_batch(kn * k_inv * kgn_ref[...]).astype(k_ref.dtype)
            k_ref[:, hd, :, pl.ds(QK_NOPE, LANES)] = per_batch(kr_rot * k_inv).astype(k_ref.dtype)
            v_ref[:, hd] = per_batch(kvg[:, (2 * j + 1) * LANES:(2 * j + 2) * LANES]).astype(v_ref.dtype)


def _attn_kernel(q_ref, k_ref, v_ref, o_ref):
    t_all = q_ref.shape[2]
    nt = (((1,), (1,)), ((), ()))

    def keys(lo, n):
        return k_ref[0, 0, pl.ds(lo, n), :]

    def vals(lo, n):
        return v_ref[0, 0, pl.ds(lo, n), :]

    def scores(start, size):
        q = q_ref[0, 0, pl.ds(start, size), :]
        s_off = lax.dot_general(q, keys(0, start), nt, preferred_element_type=F32) if start else None
        sub = min(size, ATTN_TILE)
        row = lax.broadcasted_iota(jnp.int32, (sub, sub), 0)
        col = lax.broadcasted_iota(jnp.int32, (sub, sub), 1)
        near = []
        for r in range(0, size, sub):
            qr = q[r:r + sub]
            s_in = lax.dot_general(qr, keys(start, r), nt, preferred_element_type=F32) if r else None
            s_dg = lax.dot_general(qr, keys(start + r, sub), nt, preferred_element_type=F32)
            near.append((s_in, jnp.where(col <= row, s_dg, -1e30)))
        return s_off, near

    def finish(start, size, s_off, near):
        sub = min(size, ATTN_TILE)
        maxes, p_offs = [], []
        for n, (s_in, s_dg) in enumerate(near):
            m = jnp.max(s_dg, axis=-1, keepdims=True)
            if s_in is not None:
                m = jnp.maximum(m, jnp.max(s_in, axis=-1, keepdims=True))
            if s_off is not None:
                so = s_off[n * sub:(n + 1) * sub]
                m = jnp.maximum(m, jnp.max(so, axis=-1, keepdims=True))
                p_offs.append(jnp.exp(so - m))
            maxes.append(m)
        if s_off is not None:
            p_off = jnp.concatenate(p_offs, axis=0) if len(p_offs) > 1 else p_offs[0]
            acc_off = jnp.dot(p_off.astype(BF16), vals(0, start), preferred_element_type=F32)
        for n, (s_in, s_dg) in enumerate(near):
            r = n * sub
            m = maxes[n]
            p_dg = jnp.exp(s_dg - m)
            l = jnp.sum(p_dg, axis=-1, keepdims=True)
            acc = jnp.dot(p_dg.astype(BF16), vals(start + r, sub), preferred_element_type=F32)
            if s_in is not None:
                p_in = jnp.exp(s_in - m)
                l = l + jnp.sum(p_in, axis=-1, keepdims=True)
                acc = acc + jnp.dot(p_in.astype(BF16), vals(start, r), preferred_element_type=F32)
            if s_off is not None:
                l = l + jnp.sum(p_offs[n], axis=-1, keepdims=True)
                acc = acc + acc_off[r:r + sub]
            o_ref[0, pl.ds(start + r, sub), :] = (acc / l).astype(o_ref.dtype)

    tiles = [(s, ATTN_MACRO) for s in range(0, t_all - ATTN_MACRO + 1, ATTN_MACRO)]
    done = len(tiles) * ATTN_MACRO
    if done < t_all:
        tiles.append((done, t_all - done))
    pending = scores(*tiles[0])
    for idx, (start, size) in enumerate(tiles):
        nxt = scores(*tiles[idx + 1]) if idx + 1 < len(tiles) else None
        finish(start, size, *pending)
        pending = nxt


def _pack_kernel(xa_ref, xb_ref, xc_ref, meta_ref, out_ref, stage):
    nb, steps, d = xa_ref.shape
    for part, x_ref in enumerate((xa_ref, xb_ref, xc_ref)):
        base = part * steps * nb
        for b in range(nb):
            xb = x_ref[b]
            for j in range(stage.shape[0]):
                stage[j, pl.ds(base + b, steps, stride=nb), :] = xb[:, j * LANES:(j + 1) * LANES]
    out_ref[...] = jnp.concatenate([stage[j] for j in range(stage.shape[0])], axis=1)

    @pl.when(pl.program_id(0) == 0)
    def _():
        for t in range(steps):
            out_ref[pl.ds(t * nb, nb), :] = jnp.broadcast_to(meta_ref[pl.ds(t, 1), :], (nb, d))


def _unpack_kernel(*refs):
    in_refs, out_ref, stage = refs[:-2], refs[-2], refs[-1]
    nb = out_ref.shape[0]
    rows = in_refs[0].shape[0]
    steps = rows // nb
    for part, h_ref in enumerate(in_refs):
        for j in range(stage.shape[0]):
            stage[j] = h_ref[:, j * LANES:(j + 1) * LANES]
        for b in range(nb):
            out_ref[b, pl.ds(part * steps, steps), :] = jnp.concatenate(
                [stage[j, pl.ds(b, steps, stride=nb), :] for j in range(stage.shape[0])], axis=1)


def _const_spec(shape):
    nd = len(shape)
    return pl.BlockSpec(shape, lambda *_: (0,) * nd)


def _layer_spec(arr, layer):
    nd = arr.ndim - 1
    return pl.BlockSpec((None,) + arr.shape[1:], lambda *_: (layer,) + (0,) * nd)


def _pack_call(x, meta):
    b, seq, d = x.shape
    steps = PACK_STEPS
    assert meta.shape[0] == steps and STEPS_PER_TILE == 3 * steps and seq % steps == 0
    n_tiles = (seq + steps) // STEPS_PER_TILE
    rows = _tile_rows()

    def x_block(shift):
        return pl.BlockSpec((b, steps, d), lambda t: (0, jnp.maximum(3 * t + shift, 0), 0))

    return pl.pallas_call(
        _pack_kernel,
        grid=(n_tiles,),
        in_specs=[x_block(-1), x_block(0), x_block(1), _const_spec(meta.shape)],
        out_specs=pl.BlockSpec((rows, d), lambda t: (t, 0)),
        out_shape=jax.ShapeDtypeStruct((n_tiles * rows, d), x.dtype),
        scratch_shapes=[pltpu.VMEM((d // LANES, rows, LANES), F32)],
        compiler_params=_params(1),
        name="pack_time_major",
    )(x, x, x, meta)


def _unpack_call(h, b, seq):
    n_rows, d = h.shape
    rows = PACK_STEPS * b
    parts = UNPACK_BLOCKS
    assert seq % (parts * PACK_STEPS) == 0

    def h_block(k):
        return pl.BlockSpec((rows, d), lambda t: (parts * t + 1 + k, 0))

    return pl.pallas_call(
        _unpack_kernel,
        grid=(seq // (parts * PACK_STEPS),),
        in_specs=[h_block(k) for k in range(parts)],
        out_specs=pl.BlockSpec((b, parts * PACK_STEPS, d), lambda t: (0, t, 0)),
        out_shape=jax.ShapeDtypeStruct((b, seq, d), h.dtype),
        scratch_shapes=[pltpu.VMEM((d // LANES, rows, LANES), F32)],
        compiler_params=_params(1),
        name="unpack_batch_major",
    )(*([h] * parts))


def _params(n_axes, vmem_limit=VMEM_LIMIT):
    return pltpu.CompilerParams(dimension_semantics=("arbitrary",) * n_axes, vmem_limit_bytes=vmem_limit)


def _tile_rows():
    return STEPS_PER_TILE * SUBLANES


def _mixer_call(h, o, layer, w_all, stacked):
    n_rows, d = h.shape
    ng, wm, rcw, rcb, wa, ba, wx, bx, lam, wro, scw, wso, gb, wao, wout = stacked
    d_rnn = wro.shape[1]
    d_sc = wso.shape[1]
    assert 2 * d_rnn == W_BLOCK and 2 * d_sc == W_BLOCK, "rnn | conv column groups must be whole W_BLOCKs"
    rows = _tile_rows()
    row_spec = pl.BlockSpec((rows, d), lambda t: (t, 0))
    o_spec = pl.BlockSpec((SUBLANES, STEPS_PER_TILE, d), lambda t: (0, t, 0))

    def w_col_block(j):
        return pl.BlockSpec((None, w_all.shape[1], W_BLOCK), lambda t: (layer, 0, j))

    tail = (wm, rcw, rcb, wa, ba, wx, bx, lam, wro, scw, wso, gb, wao, wout)
    return pl.pallas_call(
        _mixer_kernel,
        grid=(n_rows // rows,),
        in_specs=[row_spec, o_spec, _layer_spec(ng, layer), w_col_block(0), w_col_block(1), w_col_block(2)]
        + [_layer_spec(c, layer) for c in tail],
        out_specs=row_spec,
        out_shape=jax.ShapeDtypeStruct((n_rows, d), F32),
        scratch_shapes=[pltpu.VMEM((rows + (RNN_CONV - 1) * SUBLANES, d_rnn), F32),
                        pltpu.VMEM((rows, d_rnn), F32), pltpu.VMEM((SUBLANES, d_rnn), F32),
                        pltpu.VMEM((rows + (SC_CONV - 1) * SUBLANES, d_sc), F32),
                        pltpu.VMEM((d // LANES, rows, LANES), F32)],
        compiler_params=_params(1, MIXER_VMEM_LIMIT),
        name="mixers_merge",
    )(h, o, ng, w_all, w_all, w_all, *tail)


def _qkv_call(h, layer, stacked, tabs):
    n_rows, d = h.shape
    rows = _tile_rows()
    t_all = n_rows // SUBLANES
    n_lat_cols = stacked[1].shape[2]
    tab_spec = pl.BlockSpec((STEPS_PER_TILE, tabs.shape[1]), lambda t: (t, 0))
    qk_spec = pl.BlockSpec((SUBLANES, MLA_HEADS, STEPS_PER_TILE, 2 * LANES), lambda t: (0, 0, t, 0))
    v_spec = pl.BlockSpec((SUBLANES, MLA_HEADS, STEPS_PER_TILE, V_HEAD), lambda t: (0, 0, t, 0))
    return pl.pallas_call(
        _qkv_kernel,
        grid=(n_rows // rows,),
        in_specs=[pl.BlockSpec((rows, d), lambda t: (t, 0))] + [_layer_spec(c, layer) for c in stacked]
        + [tab_spec],
        out_specs=[qk_spec, qk_spec, v_spec],
        out_shape=[jax.ShapeDtypeStruct((SUBLANES, MLA_HEADS, t_all, 2 * LANES), BF16),
                   jax.ShapeDtypeStruct((SUBLANES, MLA_HEADS, t_all, 2 * LANES), BF16),
                   jax.ShapeDtypeStruct((SUBLANES, MLA_HEADS, t_all, V_HEAD), BF16)],
        scratch_shapes=[pltpu.VMEM((n_lat_cols // LANES, rows, LANES), F32)],
        compiler_params=_params(1),
        name="mla_qkv",
    )(h, *stacked, tabs)


def _attn_call(q, k, v):
    b, nh, t_all, dk = q.shape
    return pl.pallas_call(
        _attn_kernel,
        grid=(b, nh),
        in_specs=[pl.BlockSpec((1, 1, t_all, dk), lambda bi, hi: (bi, hi, 0, 0)),
                  pl.BlockSpec((1, 1, t_all, dk), lambda bi, hi: (bi, hi, 0, 0)),
                  pl.BlockSpec((1, 1, t_all, V_HEAD), lambda bi, hi: (bi, hi, 0, 0))],
        out_specs=pl.BlockSpec((1, t_all, V_HEAD), lambda bi, hi: (bi, 0, hi)),
        out_shape=jax.ShapeDtypeStruct((b, t_all, nh * V_HEAD), BF16),
        compiler_params=_params(2),
        name="mla_attention",
    )(q, k, v)


def _rope_tables(t_all):
    inv = ROPE_THETA ** (-jnp.arange(0, QK_ROPE, 2, dtype=F32) / QK_ROPE)
    ang = jnp.arange(t_all, dtype=F32)[:, None] * inv[None, :]
    c, s = jnp.cos(ang), jnp.sin(ang)
    return jnp.concatenate([c, c, s, s, c, c, c, c, -s, s, -s, s], axis=1)


def _rotate_half(a):
    half = a.shape[-1] // 2
    return jnp.concatenate([-a[..., half:], a[..., :half]], axis=-1)


def _swap_halves(a):
    half = a.shape[-1] // 2
    return jnp.concatenate([a[..., half:], a[..., :half]], axis=-1)


def kernel(x, meta, norm_g, w_in, rg_conv_w, rg_conv_b, rg_wa, rg_ba, rg_wx, rg_bx, rg_lambda, rg_out,
           sc_conv_w, sc_out, mla_cq_g, mla_w_uq, mla_ckv_g, mla_w_uk, mla_w_uv, mla_qnorm_g,
           mla_knorm_g, mla_out, gate_b, w_out):
    b, seq, d = x.shape
    depth = norm_g.shape[0]
    d_rnn = rg_out.shape[1]
    d_sc = sc_out.shape[1]
    t_all = N_META + seq
    assert b == SUBLANES, "time-major layout needs the batch to fill one f32 sublane tile"
    assert t_all % STEPS_PER_TILE == 0

    h = _pack_call(x, meta.astype(x.dtype))
    tabs = _rope_tables(t_all)

    o_sc = 2 * d_rnn
    o_cq = o_sc + 4 * d_sc
    o_kr = o_cq + Q_LORA + KV_LORA
    o_ga = o_kr + QK_ROPE
    n_grp = MLA_HEADS // HEAD_GROUP

    row = lambda a: a.reshape(depth, 1, -1)
    w_all = w_in[:, :, :o_cq].astype(BF16)
    w_kr = w_in[:, :, o_kr:o_ga]
    w_lat = jnp.concatenate([w_in[:, :, o_cq:o_kr], w_kr, w_kr], axis=2).astype(BF16)
    w_mg = w_in[:, :, o_ga:].astype(BF16)

    wq = mla_w_uq.reshape(depth, Q_LORA, MLA_HEADS, QK_HEAD)
    wq_rope = wq[..., QK_NOPE:]
    wq_heads = jnp.concatenate([wq[..., :QK_NOPE], wq_rope, _rotate_half(wq_rope)], axis=3)
    wq_grp = wq_heads.reshape(depth, Q_LORA, n_grp, HEAD_GROUP * 2 * LANES).transpose(0, 2, 1, 3).astype(BF16)
    wk = mla_w_uk.reshape(depth, KV_LORA, MLA_HEADS, QK_NOPE)
    wv = mla_w_uv.reshape(depth, KV_LORA, MLA_HEADS, V_HEAD)
    wkv_grp = jnp.concatenate([wk, wv], axis=3).reshape(
        depth, KV_LORA, n_grp, HEAD_GROUP * 2 * LANES).transpose(0, 2, 1, 3).astype(BF16)
    qg_rope, kg_rope = mla_qnorm_g[:, QK_NOPE:], mla_knorm_g[:, QK_NOPE:]

    ng = row(norm_g)
    qkv_params = (ng, w_lat, row(mla_cq_g), row(mla_ckv_g), wq_grp, wkv_grp,
                  row(mla_qnorm_g[:, :QK_NOPE]), row(jnp.concatenate([qg_rope, _swap_halves(qg_rope)], axis=1)),
                  row(mla_knorm_g[:, :QK_NOPE]), row(jnp.concatenate([kg_rope, kg_rope], axis=1)))
    mixer_params = (ng, w_mg, rg_conv_w, row(rg_conv_b), rg_wa.astype(BF16), row(rg_ba), rg_wx.astype(BF16),
                    row(rg_bx), row(rg_lambda), rg_out.astype(BF16), sc_conv_w, sc_out.astype(BF16),
                    row(gate_b), mla_out.astype(BF16), w_out.astype(BF16))

    for l in range(depth):
        q, k, v = _qkv_call(h, l, qkv_params, tabs)
        o = _attn_call(q, k, v)
        h = _mixer_call(h, o, l, w_all, mixer_params)
    return _unpack_call(h, b, seq)
```

```python
import jax
import jax.numpy as jnp
from jax import lax
from jax.experimental import pallas as pl
from jax.experimental.pallas import tpu as pltpu

N_META = 16
EPS = 1e-6
RNN_BLOCKS = 4
RNN_CONV = 4
LRU_C = 8.0
SC_CONV = 3
MLA_HEADS = 8
QK_NOPE = 128
QK_ROPE = 64
QK_HEAD = QK_NOPE + QK_ROPE
V_HEAD = 128
Q_LORA = 384
KV_LORA = 256
ROPE_THETA = 10000.0
LOG2_E = 1.4426950408889634

LANES = 128
SUBLANES = 8
STEPS_PER_TILE = 48
ATTN_TILE = 256
ATTN_MACRO = 512
ATTN_HEADS = 2
HEAD_GROUP = 2
W_BLOCK = 2048
PACK_STEPS = 16
UNPACK_BLOCKS = 4
VMEM_LIMIT = 56 * 1024 * 1024
MIXER_VMEM_LIMIT = 60 * 1024 * 1024

F32 = jnp.float32
BF16 = jnp.bfloat16


def _rms(x, g):
    ms = jnp.mean(x * x, axis=-1, keepdims=True)
    return x * lax.rsqrt(ms + EPS) * g


def _silu(x):
    return x * jax.nn.sigmoid(x)


def _causal_taps(buf, cur, taps, halo, rows):
    width = len(taps)
    acc = taps[width - 1] * cur
    for k in range(width - 1):
        back = (width - 1 - k) * SUBLANES
        acc = acc + taps[k] * buf[pl.ds(halo - back, rows), :]
    return acc


def _mixer_kernel(x_ref, o_ref, ng_ref, wr_ref, ws0_ref, ws1_ref, wm_ref,
                  rcw_ref, rcb_ref, wa_ref, ba_ref, wx_ref, bx_ref, lam_ref, wro_ref,
                  scw_ref, wso_ref, gb_ref, wao_ref, wout_ref,
                  out_ref, xbuf, h_buf, h_carry, cbuf, o_stage):
    rows, d = x_ref.shape
    steps = o_ref.shape[1]
    d_rnn = wro_ref.shape[0]
    d_sc = wso_ref.shape[0]
    blk = d_rnn // RNN_BLOCKS
    rnn_halo = (RNN_CONV - 1) * SUBLANES
    sc_halo = (SC_CONV - 1) * SUBLANES

    @pl.when(pl.program_id(0) == 0)
    def _():
        xbuf[pl.ds(0, rnn_halo), :] = jnp.zeros((rnn_halo, d_rnn), F32)
        cbuf[pl.ds(0, sc_halo), :] = jnp.zeros((sc_halo, d_sc), F32)
        h_carry[...] = jnp.zeros_like(h_carry)

    x = x_ref[...]
    h = _rms(x, ng_ref[...]).astype(BF16)

    zr = jnp.dot(h, wr_ref[...], preferred_element_type=F32)
    xr = zr[:, :d_rnn]
    xbuf[pl.ds(rnn_halo, rows), :] = xr
    taps = [rcw_ref[pl.ds(k, 1), :] for k in range(RNN_CONV)]
    xc = _causal_taps(xbuf, xr, taps, rnn_halo, rows) + rcb_ref[...]
    xbuf[pl.ds(0, rnn_halo), :] = xbuf[pl.ds(rows, rnn_halo), :]
    xcb = xc.astype(BF16)
    ra, ri = [], []
    for n in range(RNN_BLOCKS):
        xn = xcb[:, n * blk:(n + 1) * blk]
        ra.append(jnp.dot(xn, wa_ref[n], preferred_element_type=F32))
        ri.append(jnp.dot(xn, wx_ref[n], preferred_element_type=F32))

    zs0 = jnp.dot(h, ws0_ref[...], preferred_element_type=F32)
    zs1 = jnp.dot(h, ws1_ref[...], preferred_element_type=F32)

    r = jax.nn.sigmoid(jnp.concatenate(ra, axis=-1) + ba_ref[...])
    i = jax.nn.sigmoid(jnp.concatenate(ri, axis=-1) + bx_ref[...])
    lam = lam_ref[...]
    softplus_neg = jnp.maximum(-lam, 0.0) + jnp.log1p(jnp.exp(-jnp.abs(lam)))
    a = jnp.exp((-LRU_C) * r * softplus_neg)
    u = jnp.sqrt(1.0 - a * a) * (i * xc)

    zm = jnp.dot(h, wm_ref[...], preferred_element_type=F32)

    hc = h_carry[...]
    for t in range(rows // SUBLANES):
        sl = slice(t * SUBLANES, (t + 1) * SUBLANES)
        hc = a[sl] * hc + u[sl]
        h_buf[sl, :] = hc
    h_carry[...] = hc

    cx = zs0[:, d_sc:] * zs1[:, :d_sc]
    cbuf[pl.ds(sc_halo, rows), :] = cx
    staps = [scw_ref[pl.ds(k, 1), :] for k in range(SC_CONV)]
    conv = _causal_taps(cbuf, cx, staps, sc_halo, rows)
    cbuf[pl.ds(0, sc_halo), :] = cbuf[pl.ds(rows, sc_halo), :]
    y_sc = jnp.dot((zs0[:, :d_sc] * conv * _silu(zs1[:, d_sc:])).astype(BF16), wso_ref[...],
                   preferred_element_type=F32)

    y_rnn = jnp.dot((h_buf[...] * _silu(zr[:, d_rnn:])).astype(BF16), wro_ref[...],
                    preferred_element_type=F32)

    for b in range(SUBLANES):
        ob = o_ref[b].astype(F32)
        for j in range(o_stage.shape[0]):
            o_stage[j, pl.ds(b, steps, stride=SUBLANES), :] = ob[:, j * LANES:(j + 1) * LANES]
    o_tm = jnp.concatenate([o_stage[j] for j in range(o_stage.shape[0])], axis=1)
    y_att = jnp.dot((o_tm * _silu(zm[:, :d])).astype(BF16), wao_ref[...], preferred_element_type=F32)

    gates = jax.nn.sigmoid(zm[:, d:] + gb_ref[...])
    merged = gates[:, :d] * y_rnn + gates[:, d:2 * d] * y_sc + gates[:, 2 * d:] * y_att
    out_ref[...] = x + jnp.dot(merged.astype(BF16), wout_ref[...], preferred_element_type=F32)


def _per_batch_rows(buf, steps):
    return jnp.concatenate(
        [jnp.concatenate([buf[j, pl.ds(b, steps, stride=SUBLANES), :] for j in range(buf.shape[0])], axis=1)
         for b in range(SUBLANES)], axis=0)


def _qkv_kernel(x_ref, ng_ref, w_ref, cqg_ref, ckvg_ref, wq_ref, wkv_ref, qgn_ref, qgr_ref,
                kgn_ref, kgr_ref, tab_ref, q_ref, k_ref, v_ref, lat_buf):
    rows = x_ref.shape[0]
    steps = rows // SUBLANES
    n_lat = Q_LORA + KV_LORA
    scale = QK_HEAD ** -0.5 * LOG2_E
    inv_head = 1.0 / QK_HEAD
    ones = jnp.ones((SUBLANES, steps, LANES), v_ref.dtype)

    h = _rms(x_ref[...], ng_ref[...]).astype(BF16)
    z = jnp.dot(h, w_ref[...], preferred_element_type=F32)
    for j in range(lat_buf.shape[0]):
        lat_buf[j] = z[:, j * LANES:(j + 1) * LANES]
    zb = _per_batch_rows(lat_buf, steps)
    kr = zb[:, n_lat:]
    cq = _rms(zb[:, :Q_LORA], cqg_ref[...]).astype(BF16)
    ckv = _rms(zb[:, Q_LORA:n_lat], ckvg_ref[...]).astype(BF16)

    def tiled(j):
        return jnp.concatenate([tab_ref[:, j * LANES:(j + 1) * LANES]] * SUBLANES, axis=0)

    q_tab = tiled(0) * qgr_ref[...]
    kr_half_sq = 0.5 * kr * kr
    krg = kr * kgr_ref[...]
    kr_rot = krg * tiled(1) + pltpu.roll(krg, QK_ROPE // 2, axis=1) * tiled(2)

    def per_batch(a):
        return a.reshape(SUBLANES, steps, a.shape[-1])

    for grp in range(MLA_HEADS // HEAD_GROUP):
        qg = jnp.dot(cq, wq_ref[grp], preferred_element_type=F32)
        kvg = jnp.dot(ckv, wkv_ref[grp], preferred_element_type=F32)
        for j in range(HEAD_GROUP):
            hd = grp * HEAD_GROUP + j
            qn = qg[:, (2 * j) * LANES:(2 * j + 1) * LANES]
            qx = qg[:, (2 * j + 1) * LANES:(2 * j + 2) * LANES]
            q_ss = jnp.sum(qn * qn + 0.5 * (qx * qx), axis=-1, keepdims=True)
            q_inv = lax.rsqrt(q_ss * inv_head + EPS) * scale
            q_ref[:, hd, :, pl.ds(0, QK_NOPE)] = per_batch(qn * q_inv * qgn_ref[...]).astype(q_ref.dtype)
            q_ref[:, hd, :, pl.ds(QK_NOPE, LANES)] = per_batch(qx * q_tab * q_inv).astype(q_ref.dtype)

            kn = kvg[:, (2 * j) * LANES:(2 * j + 1) * LANES]
            k_ss = jnp.sum(kn * kn + kr_half_sq, axis=-1, keepdims=True)
            k_inv = lax.rsqrt(k_ss * inv_head + EPS)
            k_ref[:, hd, :, pl.ds(0, QK_NOPE)] = per_batch(kn * k_inv * kgn_ref[...]).astype(k_ref.dtype)
            k_ref[:, hd, :, pl.ds(QK_NOPE, LANES)] = per_batch(kr_rot * k_inv).astype(k_ref.dtype)
            v_ref[:, hd, :, pl.ds(0, V_HEAD)] = per_batch(
                kvg[:, (2 * j + 1) * LANES:(2 * j + 2) * LANES]).astype(v_ref.dtype)
            v_ref[:, hd, :, pl.ds(V_HEAD, LANES)] = ones


def _attn_kernel(q_ref, k_ref, v_ref, o_ref):
    t_all = q_ref.shape[2]
    nt = (((1,), (1,)), ((), ()))

    def scores(hd, start, size):
        def keys(lo, n):
            return k_ref[0, hd, pl.ds(lo, n), :]

        q = q_ref[0, hd, pl.ds(start, size), :]
        s_off = lax.dot_general(q, keys(0, start), nt, preferred_element_type=F32) if start else None
        sub = min(size, ATTN_TILE)
        row = lax.broadcasted_iota(jnp.int32, (sub, sub), 0)
        col = lax.broadcasted_iota(jnp.int32, (sub, sub), 1)
        near = []
        for r in range(0, size, sub):
            qr = q[r:r + sub]
            s_in = lax.dot_general(qr, keys(start, r), nt, preferred_element_type=F32) if r else None
            s_dg = lax.dot_general(qr, keys(start + r, sub), nt, preferred_element_type=F32)
            near.append((s_in, jnp.where(col <= row, s_dg, -1e30)))
        return s_off, near

    def finish(hd, start, size, s_off, near):
        def vals(lo, n):
            return v_ref[0, hd, pl.ds(lo, n), :]

        sub = min(size, ATTN_TILE)
        maxes, p_offs = [], []
        for n, (s_in, s_dg) in enumerate(near):
            m = jnp.max(s_dg, axis=-1, keepdims=True)
            if s_in is not None:
                m = jnp.maximum(m, jnp.max(s_in, axis=-1, keepdims=True))
            if s_off is not None:
                so = s_off[n * sub:(n + 1) * sub]
                m = jnp.maximum(m, jnp.max(so, axis=-1, keepdims=True))
                p_offs.append(jnp.exp2(so - m).astype(BF16))
            maxes.append(m)
        if s_off is not None:
            p_off = jnp.concatenate(p_offs, axis=0) if len(p_offs) > 1 else p_offs[0]
            acc_off = jnp.dot(p_off, vals(0, start), preferred_element_type=F32)
        for n, (s_in, s_dg) in enumerate(near):
            r = n * sub
            m = maxes[n]
            acc = jnp.dot(jnp.exp2(s_dg - m).astype(BF16), vals(start + r, sub), preferred_element_type=F32)
            if s_in is not None:
                acc = acc + jnp.dot(jnp.exp2(s_in - m).astype(BF16), vals(start, r),
                                    preferred_element_type=F32)
            if s_off is not None:
                acc = acc + acc_off[r:r + sub]
            o_ref[0, pl.ds(start + r, sub), pl.ds(hd * V_HEAD, V_HEAD)] = (
                acc[:, :V_HEAD] / acc[:, V_HEAD:]).astype(o_ref.dtype)

    tiles = [(s, ATTN_MACRO) for s in range(0, t_all - ATTN_MACRO + 1, ATTN_MACRO)]
    done = len(tiles) * ATTN_MACRO
    if done < t_all:
        tiles.append((done, t_all - done))
    heads = range(q_ref.shape[1])
    pending = [scores(hd, *tiles[0]) for hd in heads]
    for idx, (start, size) in enumerate(tiles):
        nxt = [scores(hd, *tiles[idx + 1]) for hd in heads] if idx + 1 < len(tiles) else None
        for hd in heads:
            finish(hd, start, size, *pending[hd])
        pending = nxt


def _pack_kernel(xa_ref, xb_ref, xc_ref, meta_ref, out_ref, stage):
    nb, steps, d = xa_ref.shape
    for part, x_ref in enumerate((xa_ref, xb_ref, xc_ref)):
        base = part * steps * nb
        for b in range(nb):
            xb = x_ref[b]
            for j in range(stage.shape[0]):
                stage[j, pl.ds(base + b, steps, stride=nb), :] = xb[:, j * LANES:(j + 1) * LANES]
    out_ref[...] = jnp.concatenate([stage[j] for j in range(stage.shape[0])], axis=1)

    @pl.when(pl.program_id(0) == 0)
    def _():
        for t in range(steps):
            out_ref[pl.ds(t * nb, nb), :] = jnp.broadcast_to(meta_ref[pl.ds(t, 1), :], (nb, d))


def _unpack_kernel(*refs):
    in_refs, out_ref, stage = refs[:-2], refs[-2], refs[-1]
    nb = out_ref.shape[0]
    rows = in_refs[0].shape[0]
    steps = rows // nb
    for part, h_ref in enumerate(in_refs):
        for j in range(stage.shape[0]):
            stage[j] = h_ref[:, j * LANES:(j + 1) * LANES]
        for b in range(nb):
            out_ref[b, pl.ds(part * steps, steps), :] = jnp.concatenate(
                [stage[j, pl.ds(b, steps, stride=nb), :] for j in range(stage.shape[0])], axis=1)


def _const_spec(shape):
    nd = len(shape)
    return pl.BlockSpec(shape, lambda *_: (0,) * nd)


def _layer_spec(arr, layer):
    nd = arr.ndim - 1
    return pl.BlockSpec((None,) + arr.shape[1:], lambda *_: (layer,) + (0,) * nd)


def _pack_call(x, meta):
    b, seq, d = x.shape
    steps = PACK_STEPS
    assert meta.shape[0] == steps and STEPS_PER_TILE == 3 * steps and seq % steps == 0
    n_tiles = (seq + steps) // STEPS_PER_TILE
    rows = _tile_rows()

    def x_block(shift):
        return pl.BlockSpec((b, steps, d), lambda t: (0, jnp.maximum(3 * t + shift, 0), 0))

    return pl.pallas_call(
        _pack_kernel,
        grid=(n_tiles,),
        in_specs=[x_block(-1), x_block(0), x_block(1), _const_spec(meta.shape)],
        out_specs=pl.BlockSpec((rows, d), lambda t: (t, 0)),
        out_shape=jax.ShapeDtypeStruct((n_tiles * rows, d), x.dtype),
        scratch_shapes=[pltpu.VMEM((d // LANES, rows, LANES), F32)],
        compiler_params=_params(1),
        name="pack_time_major",
    )(x, x, x, meta)


def _unpack_call(h, b, seq):
    n_rows, d = h.shape
    rows = PACK_STEPS * b
    parts = UNPACK_BLOCKS
    assert seq % (parts * PACK_STEPS) == 0

    def h_block(k):
        return pl.BlockSpec((rows, d), lambda t: (parts * t + 1 + k, 0))

    return pl.pallas_call(
        _unpack_kernel,
        grid=(seq // (parts * PACK_STEPS),),
        in_specs=[h_block(k) for k in range(parts)],
        out_specs=pl.BlockSpec((b, parts * PACK_STEPS, d), lambda t: (0, t, 0)),
        out_shape=jax.ShapeDtypeStruct((b, seq, d), h.dtype),
        scratch_shapes=[pltpu.VMEM((d // LANES, rows, LANES), F32)],
        compiler_params=_params(1),
        name="unpack_batch_major",
    )(*([h] * parts))


def _params(n_axes, vmem_limit=VMEM_LIMIT):
    return pltpu.CompilerParams(dimension_semantics=("arbitrary",) * n_axes, vmem_limit_bytes=vmem_limit)


def _tile_rows():
    return STEPS_PER_TILE * SUBLANES


def _mixer_call(h, o, layer, w_all, stacked):
    n_rows, d = h.shape
    ng, wm, rcw, rcb, wa, ba, wx, bx, lam, wro, scw, wso, gb, wao, wout = stacked
    d_rnn = wro.shape[1]
    d_sc = wso.shape[1]
    assert 2 * d_rnn == W_BLOCK and 2 * d_sc == W_BLOCK, "rnn | conv column groups must be whole W_BLOCKs"
    rows = _tile_rows()
    row_spec = pl.BlockSpec((rows, d), lambda t: (t, 0))
    o_spec = pl.BlockSpec((SUBLANES, STEPS_PER_TILE, d), lambda t: (0, t, 0))

    def w_col_block(j):
        return pl.BlockSpec((None, w_all.shape[1], W_BLOCK), lambda t: (layer, 0, j))

    tail = (wm, rcw, rcb, wa, ba, wx, bx, lam, wro, scw, wso, gb, wao, wout)
    return pl.pallas_call(
        _mixer_kernel,
        grid=(n_rows // rows,),
        in_specs=[row_spec, o_spec, _layer_spec(ng, layer), w_col_block(0), w_col_block(1), w_col_block(2)]
        + [_layer_spec(c, layer) for c in tail],
        out_specs=row_spec,
        out_shape=jax.ShapeDtypeStruct((n_rows, d), F32),
        scratch_shapes=[pltpu.VMEM((rows + (RNN_CONV - 1) * SUBLANES, d_rnn), F32),
                        pltpu.VMEM((rows, d_rnn), F32), pltpu.VMEM((SUBLANES, d_rnn), F32),
                        pltpu.VMEM((rows + (SC_CONV - 1) * SUBLANES, d_sc), F32),
                        pltpu.VMEM((d // LANES, rows, LANES), F32)],
        compiler_params=_params(1, MIXER_VMEM_LIMIT),
        name="mixers_merge",
    )(h, o, ng, w_all, w_all, w_all, *tail)


def _qkv_call(h, layer, stacked, tabs):
    n_rows, d = h.shape
    rows = _tile_rows()
    t_all = n_rows // SUBLANES
    n_lat_cols = stacked[1].shape[2]
    tab_spec = pl.BlockSpec((STEPS_PER_TILE, tabs.shape[1]), lambda t: (t, 0))
    qk_spec = pl.BlockSpec((SUBLANES, MLA_HEADS, STEPS_PER_TILE, 2 * LANES), lambda t: (0, 0, t, 0))
    v_spec = pl.BlockSpec((SUBLANES, MLA_HEADS, STEPS_PER_TILE, V_HEAD + LANES), lambda t: (0, 0, t, 0))
    return pl.pallas_call(
        _qkv_kernel,
        grid=(n_rows // rows,),
        in_specs=[pl.BlockSpec((rows, d), lambda t: (t, 0))] + [_layer_spec(c, layer) for c in stacked]
        + [tab_spec],
        out_specs=[qk_spec, qk_spec, v_spec],
        out_shape=[jax.ShapeDtypeStruct((SUBLANES, MLA_HEADS, t_all, 2 * LANES), BF16),
                   jax.ShapeDtypeStruct((SUBLANES, MLA_HEADS, t_all, 2 * LANES), BF16),
                   jax.ShapeDtypeStruct((SUBLANES, MLA_HEADS, t_all, V_HEAD + LANES), BF16)],
        scratch_shapes=[pltpu.VMEM((n_lat_cols // LANES, rows, LANES), F32)],
        compiler_params=_params(1),
        name="mla_qkv",
    )(h, *stacked, tabs)


def _attn_call(q, k, v):
    b, nh, t_all, dk = q.shape
    return pl.pallas_call(
        _attn_kernel,
        grid=(b, nh // ATTN_HEADS),
        in_specs=[pl.BlockSpec((1, ATTN_HEADS, t_all, dk), lambda bi, hi: (bi, hi, 0, 0)),
                  pl.BlockSpec((1, ATTN_HEADS, t_all, dk), lambda bi, hi: (bi, hi, 0, 0)),
                  pl.BlockSpec((1, ATTN_HEADS, t_all, v.shape[3]), lambda bi, hi: (bi, hi, 0, 0))],
        out_specs=pl.BlockSpec((1, t_all, ATTN_HEADS * V_HEAD), lambda bi, hi: (bi, 0, hi)),
        out_shape=jax.ShapeDtypeStruct((b, t_all, nh * V_HEAD), BF16),
        compiler_params=_params(2),
        name="mla_attention",
    )(q, k, v)


def _rope_tables(t_all):
    inv = ROPE_THETA ** (-jnp.arange(0, QK_ROPE, 2, dtype=F32) / QK_ROPE)
    ang = jnp.arange(t_all, dtype=F32)[:, None] * inv[None, :]
    c, s = jnp.cos(ang), jnp.sin(ang)
    return jnp.concatenate([c, c, s, s, c, c, c, c, -s, s, -s, s], axis=1)


def _rotate_half(a):
    half = a.shape[-1] // 2
    return jnp.concatenate([-a[..., half:], a[..., :half]], axis=-1)


def _swap_halves(a):
    half = a.shape[-1] // 2
    return jnp.concatenate([a[..., half:], a[..., :half]], axis=-1)


def kernel(x, meta, norm_g, w_in, rg_conv_w, rg_conv_b, rg_wa, rg_ba, rg_wx, rg_bx, rg_lambda, rg_out,
           sc_conv_w, sc_out, mla_cq_g, mla_w_uq, mla_ckv_g, mla_w_uk, mla_w_uv, mla_qnorm_g,
           mla_knorm_g, mla_out, gate_b, w_out):
    b, seq, d = x.shape
    depth = norm_g.shape[0]
    d_rnn = rg_out.shape[1]
    d_sc = sc_out.shape[1]
    t_all = N_META + seq
    assert b == SUBLANES, "time-major layout needs the batch to fill one f32 sublane tile"
    assert t_all % STEPS_PER_TILE == 0

    h = _pack_call(x, meta.astype(x.dtype))
    tabs = _rope_tables(t_all)

    o_sc = 2 * d_rnn
    o_cq = o_sc + 4 * d_sc
    o_kr = o_cq + Q_LORA + KV_LORA
    o_ga = o_kr + QK_ROPE
    n_grp = MLA_HEADS // HEAD_GROUP

    row = lambda a: a.reshape(depth, 1, -1)
    w_all = w_in[:, :, :o_cq].astype(BF16)
    w_kr = w_in[:, :, o_kr:o_ga]
    w_lat = jnp.concatenate([w_in[:, :, o_cq:o_kr], w_kr, w_kr], axis=2).astype(BF16)
    w_mg = w_in[:, :, o_ga:].astype(BF16)

    wq = mla_w_uq.reshape(depth, Q_LORA, MLA_HEADS, QK_HEAD)
    wq_rope = wq[..., QK_NOPE:]
    wq_heads = jnp.concatenate([wq[..., :QK_NOPE], wq_rope, _rotate_half(wq_rope)], axis=3)
    wq_grp = wq_heads.reshape(depth, Q_LORA, n_grp, HEAD_GROUP * 2 * LANES).transpose(0, 2, 1, 3).astype(BF16)
    wk = mla_w_uk.reshape(depth, KV_LORA, MLA_HEADS, QK_NOPE)
    wv = mla_w_uv.reshape(depth, KV_LORA, MLA_HEADS, V_HEAD)
    wkv_grp = jnp.concatenate([wk, wv], axis=3).reshape(
        depth, KV_LORA, n_grp, HEAD_GROUP * 2 * LANES).transpose(0, 2, 1, 3).astype(BF16)
    qg_rope, kg_rope = mla_qnorm_g[:, QK_NOPE:], mla_knorm_g[:, QK_NOPE:]

    ng = row(norm_g)
    qkv_params = (ng, w_lat, row(mla_cq_g), row(mla_ckv_g), wq_grp, wkv_grp,
                  row(mla_qnorm_g[:, :QK_NOPE]), row(jnp.concatenate([qg_rope, _swap_halves(qg_rope)], axis=1)),
                  row(mla_knorm_g[:, :QK_NOPE]), row(jnp.concatenate([kg_rope, kg_rope], axis=1)))
    mixer_params = (ng, w_mg, rg_conv_w, row(rg_conv_b), rg_wa.astype(BF16), row(rg_ba), rg_wx.astype(BF16),
                    row(rg_bx), row(rg_lambda), rg_out.astype(BF16), sc_conv_w, sc_out.astype(BF16),
                    row(gate_b), mla_out.astype(BF16), w_out.astype(BF16))

    for l in range(depth):
        q, k, v = _qkv_call(h, l, qkv_params, tabs)
        o = _attn_call(q, k, v)
        h = _mixer_call(h, o, l, w_all, mixer_params)
    return _unpack_call(h, b, seq)
```

```python
import functools

import jax
import jax.numpy as jnp
from jax import lax
from jax.experimental import pallas as pl
from jax.experimental.pallas import tpu as pltpu

N_META = 16
EPS = 1e-6
RNN_BLOCKS = 4
RNN_CONV = 4
LRU_C = 8.0
SC_CONV = 3
MLA_HEADS = 8
QK_NOPE = 128
QK_ROPE = 64
QK_HEAD = QK_NOPE + QK_ROPE
V_HEAD = 128
Q_LORA = 384
KV_LORA = 256
ROPE_THETA = 10000.0
LOG2_E = 1.4426950408889634

LANES = 128
SUBLANES = 8
STEPS_PER_TILE = 48
ATTN_TILE = 256
ATTN_MACRO = 512
ATTN_HEADS = 2
HEAD_GROUP = 2
W_BLOCK = 2048
WPREP_ROWS = 256
PACK_STEPS = 16
UNPACK_BLOCKS = 4
VMEM_LIMIT = 56 * 1024 * 1024
MIXER_VMEM_LIMIT = 60 * 1024 * 1024

F32 = jnp.float32
BF16 = jnp.bfloat16


def _rms(x, g):
    ms = jnp.mean(x * x, axis=-1, keepdims=True)
    return x * lax.rsqrt(ms + EPS) * g


def _silu(x):
    return x * jax.nn.sigmoid(x)


def _causal_taps(buf, cur, taps, halo, rows):
    width = len(taps)
    acc = taps[width - 1] * cur
    for k in range(width - 1):
        back = (width - 1 - k) * SUBLANES
        acc = acc + taps[k] * buf[pl.ds(halo - back, rows), :]
    return acc


def _mixer_kernel(x_ref, o_ref, ng_ref, wr_ref, ws0_ref, ws1_ref, wm_ref,
                  rcw_ref, rcb_ref, wa_ref, ba_ref, wx_ref, bx_ref, lam_ref, wro_ref,
                  scw_ref, wso_ref, gb_ref, wao_ref, wout_ref,
                  out_ref, xbuf, h_buf, h_carry, cbuf, o_stage):
    rows, d = x_ref.shape
    steps = o_ref.shape[1]
    d_rnn = wro_ref.shape[0]
    d_sc = wso_ref.shape[0]
    blk = d_rnn // RNN_BLOCKS
    rnn_halo = (RNN_CONV - 1) * SUBLANES
    sc_halo = (SC_CONV - 1) * SUBLANES

    @pl.when(pl.program_id(0) == 0)
    def _():
        xbuf[pl.ds(0, rnn_halo), :] = jnp.zeros((rnn_halo, d_rnn), F32)
        cbuf[pl.ds(0, sc_halo), :] = jnp.zeros((sc_halo, d_sc), F32)
        h_carry[...] = jnp.zeros_like(h_carry)

    x = x_ref[...]
    h = _rms(x, ng_ref[...]).astype(BF16)

    zr = jnp.dot(h, wr_ref[...], preferred_element_type=F32)
    zs0 = jnp.dot(h, ws0_ref[...], preferred_element_type=F32)
    xr = zr[:, :d_rnn]
    xbuf[pl.ds(rnn_halo, rows), :] = xr
    taps = [rcw_ref[pl.ds(k, 1), :] for k in range(RNN_CONV)]
    xc = _causal_taps(xbuf, xr, taps, rnn_halo, rows) + rcb_ref[...]
    xbuf[pl.ds(0, rnn_halo), :] = xbuf[pl.ds(rows, rnn_halo), :]
    xcb = xc.astype(BF16)
    ra, ri = [], []
    for n in range(RNN_BLOCKS):
        xn = xcb[:, n * blk:(n + 1) * blk]
        ra.append(jnp.dot(xn, wa_ref[n], preferred_element_type=F32))
        ri.append(jnp.dot(xn, wx_ref[n], preferred_element_type=F32))

    zs1 = jnp.dot(h, ws1_ref[...], preferred_element_type=F32)

    r = jax.nn.sigmoid(jnp.concatenate(ra, axis=-1) + ba_ref[...])
    i = jax.nn.sigmoid(jnp.concatenate(ri, axis=-1) + bx_ref[...])
    lam = lam_ref[...]
    softplus_neg = jnp.maximum(-lam, 0.0) + jnp.log1p(jnp.exp(-jnp.abs(lam)))
    a = jnp.exp((-LRU_C) * r * softplus_neg)
    u = jnp.sqrt(1.0 - a * a) * (i * xc)

    zm = jnp.dot(h, wm_ref[...], preferred_element_type=F32)

    hc = h_carry[...]
    for t in range(rows // SUBLANES):
        sl = slice(t * SUBLANES, (t + 1) * SUBLANES)
        hc = a[sl] * hc + u[sl]
        h_buf[sl, :] = hc
    h_carry[...] = hc

    cx = zs0[:, d_sc:] * zs1[:, :d_sc]
    cbuf[pl.ds(sc_halo, rows), :] = cx
    staps = [scw_ref[pl.ds(k, 1), :] for k in range(SC_CONV)]
    conv = _causal_taps(cbuf, cx, staps, sc_halo, rows)
    cbuf[pl.ds(0, sc_halo), :] = cbuf[pl.ds(rows, sc_halo), :]
    y_sc = jnp.dot((zs0[:, :d_sc] * conv * _silu(zs1[:, d_sc:])).astype(BF16), wso_ref[...],
                   preferred_element_type=F32)

    y_rnn = jnp.dot((h_buf[...] * _silu(zr[:, d_rnn:])).astype(BF16), wro_ref[...],
                    preferred_element_type=F32)

    for b in range(SUBLANES):
        ob = o_ref[b].astype(F32)
        for j in range(o_stage.shape[0]):
            o_stage[j, pl.ds(b, steps, stride=SUBLANES), :] = ob[:, j * LANES:(j + 1) * LANES]
    o_tm = jnp.concatenate([o_stage[j] for j in range(o_stage.shape[0])], axis=1)
    y_att = jnp.dot((o_tm * _silu(zm[:, :d])).astype(BF16), wao_ref[...], preferred_element_type=F32)

    gates = jax.nn.sigmoid(zm[:, d:] + gb_ref[...])
    merged = gates[:, :d] * y_rnn + gates[:, d:2 * d] * y_sc + gates[:, 2 * d:] * y_att
    out_ref[...] = x + jnp.dot(merged.astype(BF16), wout_ref[...], preferred_element_type=F32)


def _per_batch_rows(buf, steps):
    return jnp.concatenate(
        [jnp.concatenate([buf[j, pl.ds(b, steps, stride=SUBLANES), :] for j in range(buf.shape[0])], axis=1)
         for b in range(SUBLANES)], axis=0)


def _qkv_kernel(x_ref, ng_ref, w_ref, cqg_ref, ckvg_ref, wq_ref, wkv_ref, qgn_ref, qgr_ref,
                kgn_ref, kgr_ref, tab_ref, q_ref, k_ref, v_ref, lat_buf):
    rows = x_ref.shape[0]
    steps = rows // SUBLANES
    n_lat = Q_LORA + KV_LORA
    scale = QK_HEAD ** -0.5 * LOG2_E
    inv_head = 1.0 / QK_HEAD
    ones = jnp.ones((SUBLANES, steps, LANES), v_ref.dtype)

    h = _rms(x_ref[...], ng_ref[...]).astype(BF16)
    z = jnp.dot(h, w_ref[...], preferred_element_type=F32)
    for j in range(lat_buf.shape[0]):
        lat_buf[j] = z[:, j * LANES:(j + 1) * LANES]
    zb = _per_batch_rows(lat_buf, steps)
    kr = zb[:, n_lat:]
    cq = _rms(zb[:, :Q_LORA], cqg_ref[...]).astype(BF16)
    ckv = _rms(zb[:, Q_LORA:n_lat], ckvg_ref[...]).astype(BF16)

    def tiled(j):
        return jnp.concatenate([tab_ref[:, j * LANES:(j + 1) * LANES]] * SUBLANES, axis=0)

    q_tab = tiled(0) * qgr_ref[...]
    kr_half_sq = 0.5 * kr * kr
    krg = kr * kgr_ref[...]
    kr_rot = krg * tiled(1) + pltpu.roll(krg, QK_ROPE // 2, axis=1) * tiled(2)

    def per_batch(a):
        return a.reshape(SUBLANES, steps, a.shape[-1])

    for grp in range(MLA_HEADS // HEAD_GROUP):
        qg = jnp.dot(cq, wq_ref[grp], preferred_element_type=F32)
        kvg = jnp.dot(ckv, wkv_ref[grp], preferred_element_type=F32)
        for j in range(HEAD_GROUP):
            hd = grp * HEAD_GROUP + j
            qn = qg[:, (2 * j) * LANES:(2 * j + 1) * LANES]
            qx = qg[:, (2 * j + 1) * LANES:(2 * j + 2) * LANES]
            q_ss = jnp.sum(qn * qn + 0.5 * (qx * qx), axis=-1, keepdims=True)
            q_inv = lax.rsqrt(q_ss * inv_head + EPS) * scale
            q_ref[:, hd, :, pl.ds(0, QK_NOPE)] = per_batch(qn * q_inv * qgn_ref[...]).astype(q_ref.dtype)
            q_ref[:, hd, :, pl.ds(QK_NOPE, LANES)] = per_batch(qx * q_tab * q_inv).astype(q_ref.dtype)

            kn = kvg[:, (2 * j) * LANES:(2 * j + 1) * LANES]
            k_ss = jnp.sum(kn * kn + kr_half_sq, axis=-1, keepdims=True)
            k_inv = lax.rsqrt(k_ss * inv_head + EPS)
            k_ref[:, hd, :, pl.ds(0, QK_NOPE)] = per_batch(kn * k_inv * kgn_ref[...]).astype(k_ref.dtype)
            k_ref[:, hd, :, pl.ds(QK_NOPE, LANES)] = per_batch(kr_rot * k_inv).astype(k_ref.dtype)
            v_ref[:, hd, :, pl.ds(0, V_HEAD)] = per_batch(
                kvg[:, (2 * j + 1) * LANES:(2 * j + 2) * LANES]).astype(v_ref.dtype)
            v_ref[:, hd, :, pl.ds(V_HEAD, LANES)] = ones


def _attn_kernel(q_ref, k_ref, v_ref, o_ref):
    t_all = q_ref.shape[2]
    nt = (((1,), (1,)), ((), ()))

    def scores(hd, start, size):
        def keys(lo, n):
            return k_ref[0, hd, pl.ds(lo, n), :]

        q = q_ref[0, hd, pl.ds(start, size), :]
        s_off = lax.dot_general(q, keys(0, start), nt, preferred_element_type=F32) if start else None
        sub = min(size, ATTN_TILE)
        row = lax.broadcasted_iota(jnp.int32, (sub, sub), 0)
        col = lax.broadcasted_iota(jnp.int32, (sub, sub), 1)
        near = []
        for r in range(0, size, sub):
            qr = q[r:r + sub]
            s_in = lax.dot_general(qr, keys(start, r), nt, preferred_element_type=F32) if r else None
            s_dg = lax.dot_general(qr, keys(start + r, sub), nt, preferred_element_type=F32)
            near.append((s_in, jnp.where(col <= row, s_dg, -1e30)))
        return s_off, near

    def finish(hd, start, size, s_off, near):
        def vals(lo, n):
            return v_ref[0, hd, pl.ds(lo, n), :]

        sub = min(size, ATTN_TILE)
        maxes, p_offs = [], []
        for n, (s_in, s_dg) in enumerate(near):
            m = jnp.max(s_dg, axis=-1, keepdims=True)
            if s_in is not None:
                m = jnp.maximum(m, jnp.max(s_in, axis=-1, keepdims=True))
            if s_off is not None:
                so = s_off[n * sub:(n + 1) * sub]
                m = jnp.maximum(m, jnp.max(so, axis=-1, keepdims=True))
                p_offs.append(jnp.exp2(so - m).astype(BF16))
            maxes.append(m)
        if s_off is not None:
            p_off = jnp.concatenate(p_offs, axis=0) if len(p_offs) > 1 else p_offs[0]
            acc_off = jnp.dot(p_off, vals(0, start), preferred_element_type=F32)
        for n, (s_in, s_dg) in enumerate(near):
            r = n * sub
            m = maxes[n]
            acc = jnp.dot(jnp.exp2(s_dg - m).astype(BF16), vals(start + r, sub), preferred_element_type=F32)
            if s_in is not None:
                acc = acc + jnp.dot(jnp.exp2(s_in - m).astype(BF16), vals(start, r),
                                    preferred_element_type=F32)
            if s_off is not None:
                acc = acc + acc_off[r:r + sub]
            o_ref[0, pl.ds(start + r, sub), pl.ds(hd * V_HEAD, V_HEAD)] = (
                acc[:, :V_HEAD] / acc[:, V_HEAD:]).astype(o_ref.dtype)

    tiles = [(s, ATTN_MACRO) for s in range(0, t_all - ATTN_MACRO + 1, ATTN_MACRO)]
    done = len(tiles) * ATTN_MACRO
    if done < t_all:
        tiles.append((done, t_all - done))
    heads = range(q_ref.shape[1])
    pending = [scores(hd, *tiles[0]) for hd in heads]
    for idx, (start, size) in enumerate(tiles):
        nxt = [scores(hd, *tiles[idx + 1]) for hd in heads] if idx + 1 < len(tiles) else None
        for hd in heads:
            finish(hd, start, size, *pending[hd])
        pending = nxt


def _pack_kernel(xa_ref, xb_ref, xc_ref, meta_ref, out_ref, stage):
    nb, steps, d = xa_ref.shape
    for part, x_ref in enumerate((xa_ref, xb_ref, xc_ref)):
        base = part * steps * nb
        for b in range(nb):
            xb = x_ref[b]
            for j in range(stage.shape[0]):
                stage[j, pl.ds(base + b, steps, stride=nb), :] = xb[:, j * LANES:(j + 1) * LANES]
    out_ref[...] = jnp.concatenate([stage[j] for j in range(stage.shape[0])], axis=1)

    @pl.when(pl.program_id(0) == 0)
    def _():
        for t in range(steps):
            out_ref[pl.ds(t * nb, nb), :] = jnp.broadcast_to(meta_ref[pl.ds(t, 1), :], (nb, d))


def _unpack_kernel(*refs):
    in_refs, out_ref, stage = refs[:-2], refs[-2], refs[-1]
    nb = out_ref.shape[0]
    rows = in_refs[0].shape[0]
    steps = rows // nb
    for part, h_ref in enumerate(in_refs):
        for j in range(stage.shape[0]):
            stage[j] = h_ref[:, j * LANES:(j + 1) * LANES]
        for b in range(nb):
            out_ref[b, pl.ds(part * steps, steps), :] = jnp.concatenate(
                [stage[j, pl.ds(b, steps, stride=nb), :] for j in range(stage.shape[0])], axis=1)


def _wprep_kernel(w_ref, wall_ref, wmg_ref, wlat_ref, *, o_cq, o_kr, o_ga):
    wall_ref[...] = w_ref[:, pl.ds(0, o_cq)].astype(wall_ref.dtype)
    n_lat = o_kr - o_cq
    wlat_ref[:, pl.ds(0, n_lat)] = w_ref[:, pl.ds(o_cq, n_lat)].astype(wlat_ref.dtype)
    tail = w_ref[:, pl.ds(o_kr, w_ref.shape[1] - o_kr)]
    kr = tail[:, :o_ga - o_kr]
    wlat_ref[:, pl.ds(n_lat, 2 * (o_ga - o_kr))] = jnp.concatenate([kr, kr], axis=1).astype(wlat_ref.dtype)
    wmg_ref[...] = tail[:, o_ga - o_kr:].astype(wmg_ref.dtype)


def _const_spec(shape):
    nd = len(shape)
    return pl.BlockSpec(shape, lambda *_: (0,) * nd)


def _layer_spec(arr, layer):
    nd = arr.ndim - 1
    return pl.BlockSpec((None,) + arr.shape[1:], lambda *_: (layer,) + (0,) * nd)


def _wprep_call(w_in, o_cq, o_kr, o_ga):
    depth, d, n_in = w_in.shape
    rows = WPREP_ROWS
    n_lat = o_kr - o_cq + 2 * (o_ga - o_kr)
    assert o_cq % LANES == 0 and o_kr % LANES == 0 and n_lat % LANES == 0 and d % rows == 0

    def spec(width):
        return pl.BlockSpec((None, rows, width), lambda l, r: (l, r, 0))

    return pl.pallas_call(
        functools.partial(_wprep_kernel, o_cq=o_cq, o_kr=o_kr, o_ga=o_ga),
        grid=(depth, d // rows),
        in_specs=[spec(n_in)],
        out_specs=[spec(o_cq), spec(n_in - o_ga), spec(n_lat)],
        out_shape=[jax.ShapeDtypeStruct((depth, d, o_cq), BF16),
                   jax.ShapeDtypeStruct((depth, d, n_in - o_ga), BF16),
                   jax.ShapeDtypeStruct((depth, d, n_lat), BF16)],
        compiler_params=_params(2),
        name="w_in_prep",
    )(w_in)


def _pack_call(x, meta):
    b, seq, d = x.shape
    steps = PACK_STEPS
    assert meta.shape[0] == steps and STEPS_PER_TILE == 3 * steps and seq % steps == 0
    n_tiles = (seq + steps) // STEPS_PER_TILE
    rows = _tile_rows()

    def x_block(shift):
        return pl.BlockSpec((b, steps, d), lambda t: (0, jnp.maximum(3 * t + shift, 0), 0))

    return pl.pallas_call(
        _pack_kernel,
        grid=(n_tiles,),
        in_specs=[x_block(-1), x_block(0), x_block(1), _const_spec(meta.shape)],
        out_specs=pl.BlockSpec((rows, d), lambda t: (t, 0)),
        out_shape=jax.ShapeDtypeStruct((n_tiles * rows, d), x.dtype),
        scratch_shapes=[pltpu.VMEM((d // LANES, rows, LANES), F32)],
        compiler_params=_params(1),
        name="pack_time_major",
    )(x, x, x, meta)


def _unpack_call(h, b, seq):
    n_rows, d = h.shape
    rows = PACK_STEPS * b
    parts = UNPACK_BLOCKS
    assert seq % (parts * PACK_STEPS) == 0

    def h_block(k):
        return pl.BlockSpec((rows, d), lambda t: (parts * t + 1 + k, 0))

    return pl.pallas_call(
        _unpack_kernel,
        grid=(seq // (parts * PACK_STEPS),),
        in_specs=[h_block(k) for k in range(parts)],
        out_specs=pl.BlockSpec((b, parts * PACK_STEPS, d), lambda t: (0, t, 0)),
        out_shape=jax.ShapeDtypeStruct((b, seq, d), h.dtype),
        scratch_shapes=[pltpu.VMEM((d // LANES, rows, LANES), F32)],
        compiler_params=_params(1),
        name="unpack_batch_major",
    )(*([h] * parts))


def _params(n_axes, vmem_limit=VMEM_LIMIT):
    return pltpu.CompilerParams(dimension_semantics=("arbitrary",) * n_axes, vmem_limit_bytes=vmem_limit)


def _tile_rows():
    return STEPS_PER_TILE * SUBLANES


def _mixer_call(h, o, layer, w_all, stacked):
    n_rows, d = h.shape
    ng, wm, rcw, rcb, wa, ba, wx, bx, lam, wro, scw, wso, gb, wao, wout = stacked
    d_rnn = wro.shape[1]
    d_sc = wso.shape[1]
    assert 2 * d_rnn == W_BLOCK and 2 * d_sc == W_BLOCK, "rnn | conv column groups must be whole W_BLOCKs"
    rows = _tile_rows()
    row_spec = pl.BlockSpec((rows, d), lambda t: (t, 0))
    o_spec = pl.BlockSpec((SUBLANES, STEPS_PER_TILE, d), lambda t: (0, t, 0))

    def w_col_block(j):
        return pl.BlockSpec((None, w_all.shape[1], W_BLOCK), lambda t: (layer, 0, j))

    tail = (wm, rcw, rcb, wa, ba, wx, bx, lam, wro, scw, wso, gb, wao, wout)
    return pl.pallas_call(
        _mixer_kernel,
        grid=(n_rows // rows,),
        in_specs=[row_spec, o_spec, _layer_spec(ng, layer), w_col_block(0), w_col_block(1), w_col_block(2)]
        + [_layer_spec(c, layer) for c in tail],
        out_specs=row_spec,
        out_shape=jax.ShapeDtypeStruct((n_rows, d), F32),
        scratch_shapes=[pltpu.VMEM((rows + (RNN_CONV - 1) * SUBLANES, d_rnn), F32),
                        pltpu.VMEM((rows, d_rnn), F32), pltpu.VMEM((SUBLANES, d_rnn), F32),
                        pltpu.VMEM((rows + (SC_CONV - 1) * SUBLANES, d_sc), F32),
                        pltpu.VMEM((d // LANES, rows, LANES), F32)],
        compiler_params=_params(1, MIXER_VMEM_LIMIT),
        name="mixers_merge",
    )(h, o, ng, w_all, w_all, w_all, *tail)


def _qkv_call(h, layer, stacked, tabs):
    n_rows, d = h.shape
    rows = _tile_rows()
    t_all = n_rows // SUBLANES
    n_lat_cols = stacked[1].shape[2]
    tab_spec = pl.BlockSpec((STEPS_PER_TILE, tabs.shape[1]), lambda t: (t, 0))
    qk_spec = pl.BlockSpec((SUBLANES, MLA_HEADS, STEPS_PER_TILE, 2 * LANES), lambda t: (0, 0, t, 0))
    v_spec = pl.BlockSpec((SUBLANES, MLA_HEADS, STEPS_PER_TILE, V_HEAD + LANES), lambda t: (0, 0, t, 0))
    return pl.pallas_call(
        _qkv_kernel,
        grid=(n_rows // rows,),
        in_specs=[pl.BlockSpec((rows, d), lambda t: (t, 0))] + [_layer_spec(c, layer) for c in stacked]
        + [tab_spec],
        out_specs=[qk_spec, qk_spec, v_spec],
        out_shape=[jax.ShapeDtypeStruct((SUBLANES, MLA_HEADS, t_all, 2 * LANES), BF16),
                   jax.ShapeDtypeStruct((SUBLANES, MLA_HEADS, t_all, 2 * LANES), BF16),
                   jax.ShapeDtypeStruct((SUBLANES, MLA_HEADS, t_all, V_HEAD + LANES), BF16)],
        scratch_shapes=[pltpu.VMEM((n_lat_cols // LANES, rows, LANES), F32)],
        compiler_params=_params(1),
        name="mla_qkv",
    )(h, *stacked, tabs)


def _attn_call(q, k, v):
    b, nh, t_all, dk = q.shape
    return pl.pallas_call(
        _attn_kernel,
        grid=(b, nh // ATTN_HEADS),
        in_specs=[pl.BlockSpec((1, ATTN_HEADS, t_all, dk), lambda bi, hi: (bi, hi, 0, 0)),
                  pl.BlockSpec((1, ATTN_HEADS, t_all, dk), lambda bi, hi: (bi, hi, 0, 0)),
                  pl.BlockSpec((1, ATTN_HEADS, t_all, v.shape[3]), lambda bi, hi: (bi, hi, 0, 0))],
        out_specs=pl.BlockSpec((1, t_all, ATTN_HEADS * V_HEAD), lambda bi, hi: (bi, 0, hi)),
        out_shape=jax.ShapeDtypeStruct((b, t_all, nh * V_HEAD), BF16),
        compiler_params=_params(2),
        name="mla_attention",
    )(q, k, v)


def _rope_tables(t_all):
    inv = ROPE_THETA ** (-jnp.arange(0, QK_ROPE, 2, dtype=F32) / QK_ROPE)
    ang = jnp.arange(t_all, dtype=F32)[:, None] * inv[None, :]
    c, s = jnp.cos(ang), jnp.sin(ang)
    return jnp.concatenate([c, c, s, s, c, c, c, c, -s, s, -s, s], axis=1)


def _rotate_half(a):
    half = a.shape[-1] // 2
    return jnp.concatenate([-a[..., half:], a[..., :half]], axis=-1)


def _swap_halves(a):
    half = a.shape[-1] // 2
    return jnp.concatenate([a[..., half:], a[..., :half]], axis=-1)


def kernel(x, meta, norm_g, w_in, rg_conv_w, rg_conv_b, rg_wa, rg_ba, rg_wx, rg_bx, rg_lambda, rg_out,
           sc_conv_w, sc_out, mla_cq_g, mla_w_uq, mla_ckv_g, mla_w_uk, mla_w_uv, mla_qnorm_g,
           mla_knorm_g, mla_out, gate_b, w_out):
    b, seq, d = x.shape
    depth = norm_g.shape[0]
    d_rnn = rg_out.shape[1]
    d_sc = sc_out.shape[1]
    t_all = N_META + seq
    assert b == SUBLANES, "time-major layout needs the batch to fill one f32 sublane tile"
    assert t_all % STEPS_PER_TILE == 0

    h = _pack_call(x, meta.astype(x.dtype))
    tabs = _rope_tables(t_all)

    o_sc = 2 * d_rnn
    o_cq = o_sc + 4 * d_sc
    o_kr = o_cq + Q_LORA + KV_LORA
    o_ga = o_kr + QK_ROPE
    n_grp = MLA_HEADS // HEAD_GROUP

    row = lambda a: a.reshape(depth, 1, -1)
    w_all, w_mg, w_lat = _wprep_call(w_in, o_cq, o_kr, o_ga)

    wq = mla_w_uq.reshape(depth, Q_LORA, MLA_HEADS, QK_HEAD)
    wq_rope = wq[..., QK_NOPE:]
    wq_heads = jnp.concatenate([wq[..., :QK_NOPE], wq_rope, _rotate_half(wq_rope)], axis=3)
    wq_grp = wq_heads.reshape(depth, Q_LORA, n_grp, HEAD_GROUP * 2 * LANES).transpose(0, 2, 1, 3).astype(BF16)
    wk = mla_w_uk.reshape(depth, KV_LORA, MLA_HEADS, QK_NOPE)
    wv = mla_w_uv.reshape(depth, KV_LORA, MLA_HEADS, V_HEAD)
    wkv_grp = jnp.concatenate([wk, wv], axis=3).reshape(
        depth, KV_LORA, n_grp, HEAD_GROUP * 2 * LANES).transpose(0, 2, 1, 3).astype(BF16)
    qg_rope, kg_rope = mla_qnorm_g[:, QK_NOPE:], mla_knorm_g[:, QK_NOPE:]

    ng = row(norm_g)
    qkv_params = (ng, w_lat, row(mla_cq_g), row(mla_ckv_g), wq_grp, wkv_grp,
                  row(mla_qnorm_g[:, :QK_NOPE]), row(jnp.concatenate([qg_rope, _swap_halves(qg_rope)], axis=1)),
                  row(mla_knorm_g[:, :QK_NOPE]), row(jnp.concatenate([kg_rope, kg_rope], axis=1)))
    mixer_params = (ng, w_mg, rg_conv_w, row(rg_conv_b), rg_wa.astype(BF16), row(rg_ba), rg_wx.astype(BF16),
                    row(rg_bx), row(rg_lambda), rg_out.astype(BF16), sc_conv_w, sc_out.astype(BF16),
                    row(gate_b), mla_out.astype(BF16), w_out.astype(BF16))

    for l in range(depth):
        q, k, v = _qkv_call(h, l, qkv_params, tabs)
        o = _attn_call(q, k, v)
        h = _mixer_call(h, o, l, w_all, mixer_params)
    return _unpack_call(h, b, seq)
```

```python
import jax
import jax.numpy as jnp
from jax import lax
from jax.experimental import pallas as pl
from jax.experimental.pallas import tpu as pltpu

N_META = 16
EPS = 1e-6
RNN_BLOCKS = 4
RNN_CONV = 4
LRU_C = 8.0
SC_CONV = 3
MLA_HEADS = 8
QK_NOPE = 128
QK_ROPE = 64
QK_HEAD = QK_NOPE + QK_ROPE
V_HEAD = 128
Q_LORA = 384
KV_LORA = 256
ROPE_THETA = 10000.0
LOG2_E = 1.4426950408889634

LANES = 128
SUBLANES = 8
STEPS_PER_TILE = 48
ATTN_TILE = 256
ATTN_MACRO = 512
ATTN_HEADS = 2
HEAD_GROUP = 2
W_BLOCK = 2048
PACK_STEPS = 16
UNPACK_BLOCKS = 4
VMEM_LIMIT = 56 * 1024 * 1024
MIXER_VMEM_LIMIT = 60 * 1024 * 1024

F32 = jnp.float32
BF16 = jnp.bfloat16


def _rms(x, g):
    ms = jnp.mean(x * x, axis=-1, keepdims=True)
    return x * lax.rsqrt(ms + EPS) * g


def _silu(x):
    return x * jax.nn.sigmoid(x)


def _dot_t(a, w_t):
    return lax.dot_general(a, w_t, (((1,), (1,)), ((), ())), preferred_element_type=F32)


def _causal_taps(buf, cur, taps, halo, rows):
    width = len(taps)
    acc = taps[width - 1] * cur
    for k in range(width - 1):
        back = (width - 1 - k) * SUBLANES
        acc = acc + taps[k] * buf[pl.ds(halo - back, rows), :]
    return acc


def _mixer_kernel(x_ref, o_ref, ng_ref, wr_ref, ws0_ref, ws1_ref, wm_ref,
                  rcw_ref, rcb_ref, wa_ref, ba_ref, wx_ref, bx_ref, lam_ref, wro_ref,
                  scw_ref, wso_ref, gb_ref, wao_ref, wout_ref,
                  out_ref, xbuf, h_buf, h_carry, cbuf, o_stage):
    rows, d = x_ref.shape
    steps = o_ref.shape[1]
    d_rnn = wro_ref.shape[0]
    d_sc = wso_ref.shape[0]
    blk = d_rnn // RNN_BLOCKS
    rnn_halo = (RNN_CONV - 1) * SUBLANES
    sc_halo = (SC_CONV - 1) * SUBLANES

    @pl.when(pl.program_id(0) == 0)
    def _():
        xbuf[pl.ds(0, rnn_halo), :] = jnp.zeros((rnn_halo, d_rnn), F32)
        cbuf[pl.ds(0, sc_halo), :] = jnp.zeros((sc_halo, d_sc), F32)
        h_carry[...] = jnp.zeros_like(h_carry)

    x = x_ref[...]
    h = _rms(x, ng_ref[...]).astype(BF16)

    zr = _dot_t(h, wr_ref[...])
    zs0 = _dot_t(h, ws0_ref[...])
    xr = zr[:, :d_rnn]
    xbuf[pl.ds(rnn_halo, rows), :] = xr
    taps = [rcw_ref[pl.ds(k, 1), :] for k in range(RNN_CONV)]
    xc = _causal_taps(xbuf, xr, taps, rnn_halo, rows) + rcb_ref[...]
    xbuf[pl.ds(0, rnn_halo), :] = xbuf[pl.ds(rows, rnn_halo), :]
    xcb = xc.astype(BF16)
    ra, ri = [], []
    for n in range(RNN_BLOCKS):
        xn = xcb[:, n * blk:(n + 1) * blk]
        ra.append(jnp.dot(xn, wa_ref[n], preferred_element_type=F32))
        ri.append(jnp.dot(xn, wx_ref[n], preferred_element_type=F32))

    zs1 = _dot_t(h, ws1_ref[...])

    r = jax.nn.sigmoid(jnp.concatenate(ra, axis=-1) + ba_ref[...])
    i = jax.nn.sigmoid(jnp.concatenate(ri, axis=-1) + bx_ref[...])
    lam = lam_ref[...]
    softplus_neg = jnp.maximum(-lam, 0.0) + jnp.log1p(jnp.exp(-jnp.abs(lam)))
    a = jnp.exp((-LRU_C) * r * softplus_neg)
    u = jnp.sqrt(1.0 - a * a) * (i * xc)

    zm = _dot_t(h, wm_ref[...])

    hc = h_carry[...]
    for t in range(rows // SUBLANES):
        sl = slice(t * SUBLANES, (t + 1) * SUBLANES)
        hc = a[sl] * hc + u[sl]
        h_buf[sl, :] = hc
    h_carry[...] = hc

    cx = zs0[:, d_sc:] * zs1[:, :d_sc]
    cbuf[pl.ds(sc_halo, rows), :] = cx
    staps = [scw_ref[pl.ds(k, 1), :] for k in range(SC_CONV)]
    conv = _causal_taps(cbuf, cx, staps, sc_halo, rows)
    cbuf[pl.ds(0, sc_halo), :] = cbuf[pl.ds(rows, sc_halo), :]
    y_sc = jnp.dot((zs0[:, :d_sc] * conv * _silu(zs1[:, d_sc:])).astype(BF16), wso_ref[...],
                   preferred_element_type=F32)

    y_rnn = jnp.dot((h_buf[...] * _silu(zr[:, d_rnn:])).astype(BF16), wro_ref[...],
                    preferred_element_type=F32)

    for b in range(SUBLANES):
        ob = o_ref[b].astype(F32)
        for j in range(o_stage.shape[0]):
            o_stage[j, pl.ds(b, steps, stride=SUBLANES), :] = ob[:, j * LANES:(j + 1) * LANES]
    o_tm = jnp.concatenate([o_stage[j] for j in range(o_stage.shape[0])], axis=1)
    y_att = jnp.dot((o_tm * _silu(zm[:, :d])).astype(BF16), wao_ref[...], preferred_element_type=F32)

    gates = jax.nn.sigmoid(zm[:, d:] + gb_ref[...])
    merged = gates[:, :d] * y_rnn + gates[:, d:2 * d] * y_sc + gates[:, 2 * d:] * y_att
    out_ref[...] = x + jnp.dot(merged.astype(BF16), wout_ref[...], preferred_element_type=F32)


def _per_batch_rows(buf, steps):
    return jnp.concatenate(
        [jnp.concatenate([buf[j, pl.ds(b, steps, stride=SUBLANES), :] for j in range(buf.shape[0])], axis=1)
         for b in range(SUBLANES)], axis=0)


def _qkv_kernel(x_ref, ng_ref, w_ref, cqg_ref, ckvg_ref, wq_ref, wkv_ref, qgn_ref, qgr_ref,
                kgn_ref, kgr_ref, tab_ref, q_ref, k_ref, v_ref, lat_buf):
    rows = x_ref.shape[0]
    steps = rows // SUBLANES
    n_lat = Q_LORA + KV_LORA
    scale = QK_HEAD ** -0.5 * LOG2_E
    inv_head = 1.0 / QK_HEAD
    ones = jnp.ones((SUBLANES, steps, LANES), v_ref.dtype)

    h = _rms(x_ref[...], ng_ref[...]).astype(BF16)
    z = _dot_t(h, w_ref[...])
    for j in range(lat_buf.shape[0]):
        lat_buf[j] = z[:, j * LANES:(j + 1) * LANES]
    zb = _per_batch_rows(lat_buf, steps)
    kr = zb[:, n_lat:]
    cq = _rms(zb[:, :Q_LORA], cqg_ref[...]).astype(BF16)
    ckv = _rms(zb[:, Q_LORA:n_lat], ckvg_ref[...]).astype(BF16)

    def tiled(j):
        return jnp.concatenate([tab_ref[:, j * LANES:(j + 1) * LANES]] * SUBLANES, axis=0)

    q_tab = tiled(0) * qgr_ref[...]
    kr_half_sq = 0.5 * kr * kr
    krg = kr * kgr_ref[...]
    kr_rot = krg * tiled(1) + pltpu.roll(krg, QK_ROPE // 2, axis=1) * tiled(2)

    def per_batch(a):
        return a.reshape(SUBLANES, steps, a.shape[-1])

    for grp in range(MLA_HEADS // HEAD_GROUP):
        qg = jnp.dot(cq, wq_ref[grp], preferred_element_type=F32)
        kvg = jnp.dot(ckv, wkv_ref[grp], preferred_element_type=F32)
        for j in range(HEAD_GROUP):
            hd = grp * HEAD_GROUP + j
            qn = qg[:, (2 * j) * LANES:(2 * j + 1) * LANES]
            qx = qg[:, (2 * j + 1) * LANES:(2 * j + 2) * LANES]
            q_ss = jnp.sum(qn * qn + 0.5 * (qx * qx), axis=-1, keepdims=True)
            q_inv = lax.rsqrt(q_ss * inv_head + EPS) * scale
            q_ref[:, hd, :, pl.ds(0, QK_NOPE)] = per_batch(qn * q_inv * qgn_ref[...]).astype(q_ref.dtype)
            q_ref[:, hd, :, pl.ds(QK_NOPE, LANES)] = per_batch(qx * q_tab * q_inv).astype(q_ref.dtype)

            kn = kvg[:, (2 * j) * LANES:(2 * j + 1) * LANES]
            k_ss = jnp.sum(kn * kn + kr_half_sq, axis=-1, keepdims=True)
            k_inv = lax.rsqrt(k_ss * inv_head + EPS)
            k_ref[:, hd, :, pl.ds(0, QK_NOPE)] = per_batch(kn * k_inv * kgn_ref[...]).astype(k_ref.dtype)
            k_ref[:, hd, :, pl.ds(QK_NOPE, LANES)] = per_batch(kr_rot * k_inv).astype(k_ref.dtype)
            v_ref[:, hd, :, pl.ds(0, V_HEAD)] = per_batch(
                kvg[:, (2 * j + 1) * LANES:(2 * j + 2) * LANES]).astype(v_ref.dtype)
            v_ref[:, hd, :, pl.ds(V_HEAD, LANES)] = ones


def _attn_kernel(q_ref, k_ref, v_ref, o_ref):
    t_all = q_ref.shape[2]
    nt = (((1,), (1,)), ((), ()))

    def scores(hd, start, size):
        def keys(lo, n):
            return k_ref[0, hd, pl.ds(lo, n), :]

        q = q_ref[0, hd, pl.ds(start, size), :]
        s_off = lax.dot_general(q, keys(0, start), nt, preferred_element_type=F32) if start else None
        sub = min(size, ATTN_TILE)
        row = lax.broadcasted_iota(jnp.int32, (sub, sub), 0)
        col = lax.broadcasted_iota(jnp.int32, (sub, sub), 1)
        near = []
        for r in range(0, size, sub):
            qr = q[r:r + sub]
            s_in = lax.dot_general(qr, keys(start, r), nt, preferred_element_type=F32) if r else None
            s_dg = lax.dot_general(qr, keys(start + r, sub), nt, preferred_element_type=F32)
            near.append((s_in, jnp.where(col <= row, s_dg, -1e30)))
        return s_off, near

    def finish(hd, start, size, s_off, near):
        def vals(lo, n):
            return v_ref[0, hd, pl.ds(lo, n), :]

        sub = min(size, ATTN_TILE)
        maxes, p_offs = [], []
        for n, (s_in, s_dg) in enumerate(near):
            m = jnp.max(s_dg, axis=-1, keepdims=True)
            if s_in is not None:
                m = jnp.maximum(m, jnp.max(s_in, axis=-1, keepdims=True))
            if s_off is not None:
                so = s_off[n * sub:(n + 1) * sub]
                m = jnp.maximum(m, jnp.max(so, axis=-1, keepdims=True))
                p_offs.append(jnp.exp2(so - m).astype(BF16))
            maxes.append(m)
        if s_off is not None:
            p_off = jnp.concatenate(p_offs, axis=0) if len(p_offs) > 1 else p_offs[0]
            acc_off = jnp.dot(p_off, vals(0, start), preferred_element_type=F32)
        for n, (s_in, s_dg) in enumerate(near):
            r = n * sub
            m = maxes[n]
            acc = jnp.dot(jnp.exp2(s_dg - m).astype(BF16), vals(start + r, sub), preferred_element_type=F32)
            if s_in is not None:
                acc = acc + jnp.dot(jnp.exp2(s_in - m).astype(BF16), vals(start, r),
                                    preferred_element_type=F32)
            if s_off is not None:
                acc = acc + acc_off[r:r + sub]
            o_ref[0, pl.ds(start + r, sub), pl.ds(hd * V_HEAD, V_HEAD)] = (
                acc[:, :V_HEAD] / acc[:, V_HEAD:]).astype(o_ref.dtype)

    tiles = [(s, ATTN_MACRO) for s in range(0, t_all - ATTN_MACRO + 1, ATTN_MACRO)]
    done = len(tiles) * ATTN_MACRO
    if done < t_all:
        tiles.append((done, t_all - done))
    heads = range(q_ref.shape[1])
    pending = [scores(hd, *tiles[0]) for hd in heads]
    for idx, (start, size) in enumerate(tiles):
        nxt = [scores(hd, *tiles[idx + 1]) for hd in heads] if idx + 1 < len(tiles) else None
        for hd in heads:
            finish(hd, start, size, *pending[hd])
        pending = nxt


def _pack_kernel(xa_ref, xb_ref, xc_ref, meta_ref, out_ref, stage):
    nb, steps, d = xa_ref.shape
    for part, x_ref in enumerate((xa_ref, xb_ref, xc_ref)):
        base = part * steps * nb
        for b in range(nb):
            xb = x_ref[b]
            for j in range(stage.shape[0]):
                stage[j, pl.ds(base + b, steps, stride=nb), :] = xb[:, j * LANES:(j + 1) * LANES]
    out_ref[...] = jnp.concatenate([stage[j] for j in range(stage.shape[0])], axis=1)

    @pl.when(pl.program_id(0) == 0)
    def _():
        for t in range(steps):
            out_ref[pl.ds(t * nb, nb), :] = jnp.broadcast_to(meta_ref[pl.ds(t, 1), :], (nb, d))


def _unpack_kernel(*refs):
    in_refs, out_ref, stage = refs[:-2], refs[-2], refs[-1]
    nb = out_ref.shape[0]
    rows = in_refs[0].shape[0]
    steps = rows // nb
    for part, h_ref in enumerate(in_refs):
        for j in range(stage.shape[0]):
            stage[j] = h_ref[:, j * LANES:(j + 1) * LANES]
        for b in range(nb):
            out_ref[b, pl.ds(part * steps, steps), :] = jnp.concatenate(
                [stage[j, pl.ds(b, steps, stride=nb), :] for j in range(stage.shape[0])], axis=1)


def _const_spec(shape):
    nd = len(shape)
    return pl.BlockSpec(shape, lambda *_: (0,) * nd)


def _layer_spec(arr, layer):
    nd = arr.ndim - 1
    return pl.BlockSpec((None,) + arr.shape[1:], lambda *_: (layer,) + (0,) * nd)


def _pack_call(x, meta):
    b, seq, d = x.shape
    steps = PACK_STEPS
    assert meta.shape[0] == steps and STEPS_PER_TILE == 3 * steps and seq % steps == 0
    n_tiles = (seq + steps) // STEPS_PER_TILE
    rows = _tile_rows()

    def x_block(shift):
        return pl.BlockSpec((b, steps, d), lambda t: (0, jnp.maximum(3 * t + shift, 0), 0))

    return pl.pallas_call(
        _pack_kernel,
        grid=(n_tiles,),
        in_specs=[x_block(-1), x_block(0), x_block(1), _const_spec(meta.shape)],
        out_specs=pl.BlockSpec((rows, d), lambda t: (t, 0)),
        out_shape=jax.ShapeDtypeStruct((n_tiles * rows, d), x.dtype),
        scratch_shapes=[pltpu.VMEM((d // LANES, rows, LANES), F32)],
        compiler_params=_params(1),
        name="pack_time_major",
    )(x, x, x, meta)


def _unpack_call(h, b, seq):
    n_rows, d = h.shape
    rows = PACK_STEPS * b
    parts = UNPACK_BLOCKS
    assert seq % (parts * PACK_STEPS) == 0

    def h_block(k):
        return pl.BlockSpec((rows, d), lambda t: (parts * t + 1 + k, 0))

    return pl.pallas_call(
        _unpack_kernel,
        grid=(seq // (parts * PACK_STEPS),),
        in_specs=[h_block(k) for k in range(parts)],
        out_specs=pl.BlockSpec((b, parts * PACK_STEPS, d), lambda t: (0, t, 0)),
        out_shape=jax.ShapeDtypeStruct((b, seq, d), h.dtype),
        scratch_shapes=[pltpu.VMEM((d // LANES, rows, LANES), F32)],
        compiler_params=_params(1),
        name="unpack_batch_major",
    )(*([h] * parts))


def _params(n_axes, vmem_limit=VMEM_LIMIT):
    return pltpu.CompilerParams(dimension_semantics=("arbitrary",) * n_axes, vmem_limit_bytes=vmem_limit)


def _tile_rows():
    return STEPS_PER_TILE * SUBLANES


def _mixer_call(h, o, layer, w_all, gate_rows, stacked):
    n_rows, d = h.shape
    ng, rcw, rcb, wa, ba, wx, bx, lam, wro, scw, wso, gb, wao, wout = stacked
    d_rnn = wro.shape[1]
    d_sc = wso.shape[1]
    assert 2 * d_rnn == W_BLOCK and 2 * d_sc == W_BLOCK, "rnn | conv column groups must be whole W_BLOCKs"
    rows = _tile_rows()
    row_spec = pl.BlockSpec((rows, d), lambda t: (t, 0))
    o_spec = pl.BlockSpec((SUBLANES, STEPS_PER_TILE, d), lambda t: (0, t, 0))

    def w_col_block(j):
        return pl.BlockSpec((None, W_BLOCK, w_all.shape[2]), lambda t: (layer, j, 0))

    gate_lo, gate_n = gate_rows
    wm_spec = pl.BlockSpec((pl.Squeezed(), pl.Element(gate_n), pl.Element(w_all.shape[2])),
                           lambda t: (layer, gate_lo, 0))
    tail = (rcw, rcb, wa, ba, wx, bx, lam, wro, scw, wso, gb, wao, wout)
    return pl.pallas_call(
        _mixer_kernel,
        grid=(n_rows // rows,),
        in_specs=[row_spec, o_spec, _layer_spec(ng, layer), w_col_block(0), w_col_block(1), w_col_block(2),
                  wm_spec] + [_layer_spec(c, layer) for c in tail],
        out_specs=row_spec,
        out_shape=jax.ShapeDtypeStruct((n_rows, d), F32),
        scratch_shapes=[pltpu.VMEM((rows + (RNN_CONV - 1) * SUBLANES, d_rnn), F32),
                        pltpu.VMEM((rows, d_rnn), F32), pltpu.VMEM((SUBLANES, d_rnn), F32),
                        pltpu.VMEM((rows + (SC_CONV - 1) * SUBLANES, d_sc), F32),
                        pltpu.VMEM((d // LANES, rows, LANES), F32)],
        compiler_params=_params(1, MIXER_VMEM_LIMIT),
        name="mixers_merge",
    )(h, o, ng, w_all, w_all, w_all, w_all, *tail)


def _qkv_call(h, layer, stacked, tabs):
    n_rows, d = h.shape
    rows = _tile_rows()
    t_all = n_rows // SUBLANES
    n_lat_cols = stacked[1].shape[1]
    tab_spec = pl.BlockSpec((STEPS_PER_TILE, tabs.shape[1]), lambda t: (t, 0))
    qk_spec = pl.BlockSpec((SUBLANES, MLA_HEADS, STEPS_PER_TILE, 2 * LANES), lambda t: (0, 0, t, 0))
    v_spec = pl.BlockSpec((SUBLANES, MLA_HEADS, STEPS_PER_TILE, V_HEAD + LANES), lambda t: (0, 0, t, 0))
    return pl.pallas_call(
        _qkv_kernel,
        grid=(n_rows // rows,),
        in_specs=[pl.BlockSpec((rows, d), lambda t: (t, 0))] + [_layer_spec(c, layer) for c in stacked]
        + [tab_spec],
        out_specs=[qk_spec, qk_spec, v_spec],
        out_shape=[jax.ShapeDtypeStruct((SUBLANES, MLA_HEADS, t_all, 2 * LANES), BF16),
                   jax.ShapeDtypeStruct((SUBLANES, MLA_HEADS, t_all, 2 * LANES), BF16),
                   jax.ShapeDtypeStruct((SUBLANES, MLA_HEADS, t_all, V_HEAD + LANES), BF16)],
        scratch_shapes=[pltpu.VMEM((n_lat_cols // LANES, rows, LANES), F32)],
        compiler_params=_params(1),
        name="mla_qkv",
    )(h, *stacked, tabs)


def _attn_call(q, k, v):
    b, nh, t_all, dk = q.shape
    return pl.pallas_call(
        _attn_kernel,
        grid=(b, nh // ATTN_HEADS),
        in_specs=[pl.BlockSpec((1, ATTN_HEADS, t_all, dk), lambda bi, hi: (bi, hi, 0, 0)),
                  pl.BlockSpec((1, ATTN_HEADS, t_all, dk), lambda bi, hi: (bi, hi, 0, 0)),
                  pl.BlockSpec((1, ATTN_HEADS, t_all, v.shape[3]), lambda bi, hi: (bi, hi, 0, 0))],
        out_specs=pl.BlockSpec((1, t_all, ATTN_HEADS * V_HEAD), lambda bi, hi: (bi, 0, hi)),
        out_shape=jax.ShapeDtypeStruct((b, t_all, nh * V_HEAD), BF16),
        compiler_params=_params(2),
        name="mla_attention",
    )(q, k, v)


def _rope_tables(t_all):
    inv = ROPE_THETA ** (-jnp.arange(0, QK_ROPE, 2, dtype=F32) / QK_ROPE)
    ang = jnp.arange(t_all, dtype=F32)[:, None] * inv[None, :]
    c, s = jnp.cos(ang), jnp.sin(ang)
    return jnp.concatenate([c, c, s, s, c, c, c, c, -s, s, -s, s], axis=1)


def _rotate_half(a):
    half = a.shape[-1] // 2
    return jnp.concatenate([-a[..., half:], a[..., :half]], axis=-1)


def _swap_halves(a):
    half = a.shape[-1] // 2
    return jnp.concatenate([a[..., half:], a[..., :half]], axis=-1)


def kernel(x, meta, norm_g, w_in, rg_conv_w, rg_conv_b, rg_wa, rg_ba, rg_wx, rg_bx, rg_lambda, rg_out,
           sc_conv_w, sc_out, mla_cq_g, mla_w_uq, mla_ckv_g, mla_w_uk, mla_w_uv, mla_qnorm_g,
           mla_knorm_g, mla_out, gate_b, w_out):
    b, seq, d = x.shape
    depth = norm_g.shape[0]
    d_rnn = rg_out.shape[1]
    d_sc = sc_out.shape[1]
    t_all = N_META + seq
    assert b == SUBLANES, "time-major layout needs the batch to fill one f32 sublane tile"
    assert t_all % STEPS_PER_TILE == 0

    h = _pack_call(x, meta.astype(x.dtype))
    tabs = _rope_tables(t_all)

    o_sc = 2 * d_rnn
    o_cq = o_sc + 4 * d_sc
    o_kr = o_cq + Q_LORA + KV_LORA
    o_ga = o_kr + QK_ROPE
    n_grp = MLA_HEADS // HEAD_GROUP

    row = lambda a: a.reshape(depth, 1, -1)
    w_t = jnp.swapaxes(w_in, 1, 2)
    w_all = w_t.astype(BF16)
    w_kr = w_all[:, o_kr:o_ga]
    w_lat = jnp.concatenate([w_all[:, o_cq:o_kr], w_kr, w_kr], axis=1)

    wq = mla_w_uq.reshape(depth, Q_LORA, MLA_HEADS, QK_HEAD)
    wq_rope = wq[..., QK_NOPE:]
    wq_heads = jnp.concatenate([wq[..., :QK_NOPE], wq_rope, _rotate_half(wq_rope)], axis=3)
    wq_grp = wq_heads.reshape(depth, Q_LORA, n_grp, HEAD_GROUP * 2 * LANES).transpose(0, 2, 1, 3).astype(BF16)
    wk = mla_w_uk.reshape(depth, KV_LORA, MLA_HEADS, QK_NOPE)
    wv = mla_w_uv.reshape(depth, KV_LORA, MLA_HEADS, V_HEAD)
    wkv_grp = jnp.concatenate([wk, wv], axis=3).reshape(
        depth, KV_LORA, n_grp, HEAD_GROUP * 2 * LANES).transpose(0, 2, 1, 3).astype(BF16)
    qg_rope, kg_rope = mla_qnorm_g[:, QK_NOPE:], mla_knorm_g[:, QK_NOPE:]

    ng = row(norm_g)
    qkv_params = (ng, w_lat, row(mla_cq_g), row(mla_ckv_g), wq_grp, wkv_grp,
                  row(mla_qnorm_g[:, :QK_NOPE]), row(jnp.concatenate([qg_rope, _swap_halves(qg_rope)], axis=1)),
                  row(mla_knorm_g[:, :QK_NOPE]), row(jnp.concatenate([kg_rope, kg_rope], axis=1)))
    mixer_params = (ng, rg_conv_w, row(rg_conv_b), rg_wa.astype(BF16), row(rg_ba), rg_wx.astype(BF16),
                    row(rg_bx), row(rg_lambda), rg_out.astype(BF16), sc_conv_w, sc_out.astype(BF16),
                    row(gate_b), mla_out.astype(BF16), w_out.astype(BF16))

    for l in range(depth):
        q, k, v = _qkv_call(h, l, qkv_params, tabs)
        o = _attn_call(q, k, v)
        h = _mixer_call(h, o, l, w_all, (o_ga, w_all.shape[1] - o_ga), mixer_params)
    return _unpack_call(h, b, seq)
```

```python
import jax
import jax.numpy as jnp
from jax import lax
from jax.experimental import pallas as pl
from jax.experimental.pallas import tpu as pltpu

N_META = 16
EPS = 1e-6
RNN_BLOCKS = 4
RNN_CONV = 4
LRU_C = 8.0
SC_CONV = 3
MLA_HEADS = 8
QK_NOPE = 128
QK_ROPE = 64
QK_HEAD = QK_NOPE + QK_ROPE
V_HEAD = 128
Q_LORA = 384
KV_LORA = 256
ROPE_THETA = 10000.0
LOG2_E = 1.4426950408889634

LANES = 128
SUBLANES = 8
STEPS_PER_TILE = 48
ATTN_TILE = 256
ATTN_MACRO = 512
ATTN_HEADS = 2
HEAD_GROUP = 2
W_BLOCK = 2048
PACK_STEPS = 16
UNPACK_BLOCKS = 4
VMEM_LIMIT = 56 * 1024 * 1024
MIXER_VMEM_LIMIT = 60 * 1024 * 1024

F32 = jnp.float32
BF16 = jnp.bfloat16


def _rms(x, g):
    ms = jnp.mean(x * x, axis=-1, keepdims=True)
    return x * lax.rsqrt(ms + EPS) * g


def _silu(x):
    return x * jax.nn.sigmoid(x)


def _dot_t(a, w_t):
    return lax.dot_general(a, w_t, (((1,), (1,)), ((), ())), preferred_element_type=F32)


def _causal_taps(buf, cur, taps, halo, rows):
    width = len(taps)
    acc = taps[width - 1] * cur
    for k in range(width - 1):
        back = (width - 1 - k) * SUBLANES
        acc = acc + taps[k] * buf[pl.ds(halo - back, rows), :]
    return acc


def _mixer_kernel(x_ref, o_ref, ng_ref, wr_ref, ws0_ref, ws1_ref, wm_ref,
                  rcw_ref, rcb_ref, wax_ref, ba_ref, bx_ref, lam_ref, wro_ref,
                  scw_ref, wso_ref, gb_ref, wao_ref, wout_ref,
                  out_ref, xbuf, h_buf, h_carry, cbuf, o_stage):
    rows, d = x_ref.shape
    steps = o_ref.shape[1]
    d_rnn = wro_ref.shape[0]
    d_sc = wso_ref.shape[0]
    blk = d_rnn // RNN_BLOCKS
    rnn_halo = (RNN_CONV - 1) * SUBLANES
    sc_halo = (SC_CONV - 1) * SUBLANES

    @pl.when(pl.program_id(0) == 0)
    def _():
        xbuf[pl.ds(0, rnn_halo), :] = jnp.zeros((rnn_halo, d_rnn), F32)
        cbuf[pl.ds(0, sc_halo), :] = jnp.zeros((sc_halo, d_sc), F32)
        h_carry[...] = jnp.zeros_like(h_carry)

    for b in range(SUBLANES):
        ob = o_ref[b].astype(F32)
        for j in range(o_stage.shape[0]):
            o_stage[j, pl.ds(b, steps, stride=SUBLANES), :] = ob[:, j * LANES:(j + 1) * LANES]

    x = x_ref[...]
    h = _rms(x, ng_ref[...]).astype(BF16)

    zr = _dot_t(h, wr_ref[...])
    zs0 = _dot_t(h, ws0_ref[...])
    xr = zr[:, :d_rnn]
    xbuf[pl.ds(rnn_halo, rows), :] = xr
    taps = [rcw_ref[pl.ds(k, 1), :] for k in range(RNN_CONV)]
    xc = _causal_taps(xbuf, xr, taps, rnn_halo, rows) + rcb_ref[...]
    xbuf[pl.ds(0, rnn_halo), :] = xbuf[pl.ds(rows, rnn_halo), :]
    xcb = xc.astype(BF16)
    ra, ri = [], []
    for n in range(RNN_BLOCKS):
        xn = xcb[:, n * blk:(n + 1) * blk]
        gate_n = jnp.dot(xn, wax_ref[n], preferred_element_type=F32)
        ra.append(gate_n[:, :blk])
        ri.append(gate_n[:, blk:])

    zs1 = _dot_t(h, ws1_ref[...])

    r = jax.nn.sigmoid(jnp.concatenate(ra, axis=-1) + ba_ref[...])
    i = jax.nn.sigmoid(jnp.concatenate(ri, axis=-1) + bx_ref[...])
    lam = lam_ref[...]
    softplus_neg = jnp.maximum(-lam, 0.0) + jnp.log1p(jnp.exp(-jnp.abs(lam)))
    a = jnp.exp((-LRU_C) * r * softplus_neg)
    u = jnp.sqrt(1.0 - a * a) * (i * xc)

    zm = _dot_t(h, wm_ref[...])

    hc = h_carry[...]
    for t in range(rows // SUBLANES):
        sl = slice(t * SUBLANES, (t + 1) * SUBLANES)
        hc = a[sl] * hc + u[sl]
        h_buf[sl, :] = hc
    h_carry[...] = hc

    cx = zs0[:, d_sc:] * zs1[:, :d_sc]
    cbuf[pl.ds(sc_halo, rows), :] = cx
    staps = [scw_ref[pl.ds(k, 1), :] for k in range(SC_CONV)]
    conv = _causal_taps(cbuf, cx, staps, sc_halo, rows)
    cbuf[pl.ds(0, sc_halo), :] = cbuf[pl.ds(rows, sc_halo), :]
    y_sc = jnp.dot((zs0[:, :d_sc] * conv * _silu(zs1[:, d_sc:])).astype(BF16), wso_ref[...],
                   preferred_element_type=F32)

    y_rnn = jnp.dot((h_buf[...] * _silu(zr[:, d_rnn:])).astype(BF16), wro_ref[...],
                    preferred_element_type=F32)

    o_tm = jnp.concatenate([o_stage[j] for j in range(o_stage.shape[0])], axis=1)
    y_att = jnp.dot((o_tm * _silu(zm[:, :d])).astype(BF16), wao_ref[...], preferred_element_type=F32)

    gates = jax.nn.sigmoid(zm[:, d:] + gb_ref[...])
    merged = gates[:, :d] * y_rnn + gates[:, d:2 * d] * y_sc + gates[:, 2 * d:] * y_att
    out_ref[...] = x + jnp.dot(merged.astype(BF16), wout_ref[...], preferred_element_type=F32)


def _per_batch_rows(buf, steps):
    return jnp.concatenate(
        [jnp.concatenate([buf[j, pl.ds(b, steps, stride=SUBLANES), :] for j in range(buf.shape[0])], axis=1)
         for b in range(SUBLANES)], axis=0)


def _qkv_kernel(x_ref, ng_ref, w_ref, cqg_ref, ckvg_ref, wq_ref, wkv_ref, qgn_ref, qgr_ref,
                kgr_ref, tab_ref, q_ref, k_ref, v_ref, lat_buf):
    rows = x_ref.shape[0]
    steps = rows // SUBLANES
    n_lat = Q_LORA + KV_LORA
    norm_eps = QK_HEAD * EPS
    ones = jnp.ones((SUBLANES, steps, LANES), v_ref.dtype)

    h = _rms(x_ref[...], ng_ref[...]).astype(BF16)
    z = _dot_t(h, w_ref[...])
    for j in range(lat_buf.shape[0]):
        lat_buf[j] = z[:, j * LANES:(j + 1) * LANES]
    zb = _per_batch_rows(lat_buf, steps)
    kr = zb[:, n_lat:]
    cq = _rms(zb[:, :Q_LORA], cqg_ref[...]).astype(BF16)
    ckv = _rms(zb[:, Q_LORA:n_lat], ckvg_ref[...]).astype(BF16)

    def tiled(j):
        return jnp.concatenate([tab_ref[:, j * LANES:(j + 1) * LANES]] * SUBLANES, axis=0)

    q_tab = tiled(0) * qgr_ref[...]
    kr_half_sq = 0.5 * kr * kr
    krg = kr * kgr_ref[...]
    kr_rot = krg * tiled(1) + pltpu.roll(krg, QK_ROPE // 2, axis=1) * tiled(2)

    def per_batch(a):
        return a.reshape(SUBLANES, steps, a.shape[-1])

    for grp in range(MLA_HEADS // HEAD_GROUP):
        qg = jnp.dot(cq, wq_ref[grp], preferred_element_type=F32)
        kvg = jnp.dot(ckv, wkv_ref[grp], preferred_element_type=F32)
        for j in range(HEAD_GROUP):
            hd = grp * HEAD_GROUP + j
            qn = qg[:, (2 * j) * LANES:(2 * j + 1) * LANES]
            qx = qg[:, (2 * j + 1) * LANES:(2 * j + 2) * LANES]
            q_ss = jnp.sum(qn * qn + 0.5 * (qx * qx), axis=-1, keepdims=True)
            q_inv = lax.rsqrt(q_ss + norm_eps)
            q_ref[:, hd, :, pl.ds(0, QK_NOPE)] = per_batch(qn * q_inv * qgn_ref[...]).astype(q_ref.dtype)
            q_ref[:, hd, :, pl.ds(QK_NOPE, LANES)] = per_batch(qx * q_tab * q_inv).astype(q_ref.dtype)

            kn = kvg[:, (2 * j) * LANES:(2 * j + 1) * LANES]
            k_ss = jnp.sum(kn * kn + kr_half_sq, axis=-1, keepdims=True)
            k_inv = lax.rsqrt(k_ss + norm_eps)
            k_ref[:, hd, :, pl.ds(0, QK_NOPE)] = per_batch(kn * k_inv).astype(k_ref.dtype)
            k_ref[:, hd, :, pl.ds(QK_NOPE, LANES)] = per_batch(kr_rot * k_inv).astype(k_ref.dtype)
            v_ref[:, hd, :, pl.ds(0, V_HEAD)] = per_batch(
                kvg[:, (2 * j + 1) * LANES:(2 * j + 2) * LANES]).astype(v_ref.dtype)
            v_ref[:, hd, :, pl.ds(V_HEAD, LANES)] = ones


def _attn_kernel(q_ref, k_ref, v_ref, o_ref):
    t_all = q_ref.shape[2]
    nt = (((1,), (1,)), ((), ()))

    def scores(hd, start, size):
        def keys(lo, n):
            return k_ref[0, hd, pl.ds(lo, n), :]

        q = q_ref[0, hd, pl.ds(start, size), :]
        s_off = lax.dot_general(q, keys(0, start), nt, preferred_element_type=F32) if start else None
        sub = min(size, ATTN_TILE)
        row = lax.broadcasted_iota(jnp.int32, (sub, sub), 0)
        col = lax.broadcasted_iota(jnp.int32, (sub, sub), 1)
        near = []
        for r in range(0, size, sub):
            qr = q[r:r + sub]
            s_in = lax.dot_general(qr, keys(start, r), nt, preferred_element_type=F32) if r else None
            s_dg = lax.dot_general(qr, keys(start + r, sub), nt, preferred_element_type=F32)
            near.append((s_in, jnp.where(col <= row, s_dg, -1e30)))
        return s_off, near

    def finish(hd, start, size, s_off, near):
        def vals(lo, n):
            return v_ref[0, hd, pl.ds(lo, n), :]

        sub = min(size, ATTN_TILE)
        maxes, p_offs = [], []
        for n, (s_in, s_dg) in enumerate(near):
            m = jnp.max(s_dg, axis=-1, keepdims=True)
            if s_in is not None:
                m = jnp.maximum(m, jnp.max(s_in, axis=-1, keepdims=True))
            if s_off is not None:
                so = s_off[n * sub:(n + 1) * sub]
                m = jnp.maximum(m, jnp.max(so, axis=-1, keepdims=True))
                p_offs.append(jnp.exp2(so - m).astype(BF16))
            maxes.append(m)
        if s_off is not None:
            p_off = jnp.concatenate(p_offs, axis=0) if len(p_offs) > 1 else p_offs[0]
            acc_off = jnp.dot(p_off, vals(0, start), preferred_element_type=F32)
        for n, (s_in, s_dg) in enumerate(near):
            r = n * sub
            m = maxes[n]
            acc = jnp.dot(jnp.exp2(s_dg - m).astype(BF16), vals(start + r, sub), preferred_element_type=F32)
            if s_in is not None:
                acc = acc + jnp.dot(jnp.exp2(s_in - m).astype(BF16), vals(start, r),
                                    preferred_element_type=F32)
            if s_off is not None:
                acc = acc + acc_off[r:r + sub]
            o_ref[0, pl.ds(start + r, sub), pl.ds(hd * V_HEAD, V_HEAD)] = (
                acc[:, :V_HEAD] / acc[:, V_HEAD:]).astype(o_ref.dtype)

    tiles = [(s, ATTN_MACRO) for s in range(0, t_all - ATTN_MACRO + 1, ATTN_MACRO)]
    done = len(tiles) * ATTN_MACRO
    if done < t_all:
        tiles.append((done, t_all - done))
    heads = range(q_ref.shape[1])
    pending = [scores(hd, *tiles[0]) for hd in heads]
    for idx, (start, size) in enumerate(tiles):
        nxt = [scores(hd, *tiles[idx + 1]) for hd in heads] if idx + 1 < len(tiles) else None
        for hd in heads:
            finish(hd, start, size, *pending[hd])
        pending = nxt


def _pack_kernel(xa_ref, xb_ref, xc_ref, meta_ref, out_ref, stage):
    nb, steps, d = xa_ref.shape
    for part, x_ref in enumerate((xa_ref, xb_ref, xc_ref)):
        base = part * steps * nb
        for b in range(nb):
            xb = x_ref[b]
            for j in range(stage.shape[0]):
                stage[j, pl.ds(base + b, steps, stride=nb), :] = xb[:, j * LANES:(j + 1) * LANES]
    out_ref[...] = jnp.concatenate([stage[j] for j in range(stage.shape[0])], axis=1)

    @pl.when(pl.program_id(0) == 0)
    def _():
        for t in range(steps):
            out_ref[pl.ds(t * nb, nb), :] = jnp.broadcast_to(meta_ref[pl.ds(t, 1), :], (nb, d))


def _unpack_kernel(*refs):
    in_refs, out_ref, stage = refs[:-2], refs[-2], refs[-1]
    nb = out_ref.shape[0]
    rows = in_refs[0].shape[0]
    steps = rows // nb
    for part, h_ref in enumerate(in_refs):
        for j in range(stage.shape[0]):
            stage[j] = h_ref[:, j * LANES:(j + 1) * LANES]
        for b in range(nb):
            out_ref[b, pl.ds(part * steps, steps), :] = jnp.concatenate(
                [stage[j, pl.ds(b, steps, stride=nb), :] for j in range(stage.shape[0])], axis=1)


def _const_spec(shape):
    nd = len(shape)
    return pl.BlockSpec(shape, lambda *_: (0,) * nd)


def _layer_spec(arr, layer):
    nd = arr.ndim - 1
    return pl.BlockSpec((None,) + arr.shape[1:], lambda *_: (layer,) + (0,) * nd)


def _pack_call(x, meta):
    b, seq, d = x.shape
    steps = PACK_STEPS
    assert meta.shape[0] == steps and STEPS_PER_TILE == 3 * steps and seq % steps == 0
    n_tiles = (seq + steps) // STEPS_PER_TILE
    rows = _tile_rows()

    def x_block(shift):
        return pl.BlockSpec((b, steps, d), lambda t: (0, jnp.maximum(3 * t + shift, 0), 0))

    return pl.pallas_call(
        _pack_kernel,
        grid=(n_tiles,),
        in_specs=[x_block(-1), x_block(0), x_block(1), _const_spec(meta.shape)],
        out_specs=pl.BlockSpec((rows, d), lambda t: (t, 0)),
        out_shape=jax.ShapeDtypeStruct((n_tiles * rows, d), x.dtype),
        scratch_shapes=[pltpu.VMEM((d // LANES, rows, LANES), F32)],
        compiler_params=_params(1),
        name="pack_time_major",
    )(x, x, x, meta)


def _unpack_call(h, b, seq):
    n_rows, d = h.shape
    rows = PACK_STEPS * b
    parts = UNPACK_BLOCKS
    assert seq % (parts * PACK_STEPS) == 0

    def h_block(k):
        return pl.BlockSpec((rows, d), lambda t: (parts * t + 1 + k, 0))

    return pl.pallas_call(
        _unpack_kernel,
        grid=(seq // (parts * PACK_STEPS),),
        in_specs=[h_block(k) for k in range(parts)],
        out_specs=pl.BlockSpec((b, parts * PACK_STEPS, d), lambda t: (0, t, 0)),
        out_shape=jax.ShapeDtypeStruct((b, seq, d), h.dtype),
        scratch_shapes=[pltpu.VMEM((d // LANES, rows, LANES), F32)],
        compiler_params=_params(1),
        name="unpack_batch_major",
    )(*([h] * parts))


def _params(n_axes, vmem_limit=VMEM_LIMIT):
    return pltpu.CompilerParams(dimension_semantics=("arbitrary",) * n_axes, vmem_limit_bytes=vmem_limit)


def _tile_rows():
    return STEPS_PER_TILE * SUBLANES


def _mixer_call(h, o, layer, w_all, gate_rows, stacked):
    n_rows, d = h.shape
    ng, rcw, rcb, wax, ba, bx, lam, wro, scw, wso, gb, wao, wout = stacked
    d_rnn = wro.shape[1]
    d_sc = wso.shape[1]
    assert 2 * d_rnn == W_BLOCK and 2 * d_sc == W_BLOCK, "rnn | conv column groups must be whole W_BLOCKs"
    rows = _tile_rows()
    row_spec = pl.BlockSpec((rows, d), lambda t: (t, 0))
    o_spec = pl.BlockSpec((SUBLANES, STEPS_PER_TILE, d), lambda t: (0, t, 0))

    def w_col_block(j):
        return pl.BlockSpec((None, W_BLOCK, w_all.shape[2]), lambda t: (layer, j, 0))

    gate_lo, gate_n = gate_rows
    wm_spec = pl.BlockSpec((pl.Squeezed(), pl.Element(gate_n), pl.Element(w_all.shape[2])),
                           lambda t: (layer, gate_lo, 0))
    tail = (rcw, rcb, wax, ba, bx, lam, wro, scw, wso, gb, wao, wout)
    return pl.pallas_call(
        _mixer_kernel,
        grid=(n_rows // rows,),
        in_specs=[row_spec, o_spec, _layer_spec(ng, layer), w_col_block(0), w_col_block(1), w_col_block(2),
                  wm_spec] + [_layer_spec(c, layer) for c in tail],
        out_specs=row_spec,
        out_shape=jax.ShapeDtypeStruct((n_rows, d), F32),
        scratch_shapes=[pltpu.VMEM((rows + (RNN_CONV - 1) * SUBLANES, d_rnn), F32),
                        pltpu.VMEM((rows, d_rnn), F32), pltpu.VMEM((SUBLANES, d_rnn), F32),
                        pltpu.VMEM((rows + (SC_CONV - 1) * SUBLANES, d_sc), F32),
                        pltpu.VMEM((d // LANES, rows, LANES), F32)],
        compiler_params=_params(1, MIXER_VMEM_LIMIT),
        name="mixers_merge",
    )(h, o, ng, w_all, w_all, w_all, w_all, *tail)


def _qkv_call(h, layer, stacked, tabs):
    n_rows, d = h.shape
    rows = _tile_rows()
    t_all = n_rows // SUBLANES
    n_lat_cols = stacked[1].shape[1]
    tab_spec = pl.BlockSpec((STEPS_PER_TILE, tabs.shape[1]), lambda t: (t, 0))
    qk_spec = pl.BlockSpec((SUBLANES, MLA_HEADS, STEPS_PER_TILE, 2 * LANES), lambda t: (0, 0, t, 0))
    v_spec = pl.BlockSpec((SUBLANES, MLA_HEADS, STEPS_PER_TILE, V_HEAD + LANES), lambda t: (0, 0, t, 0))
    return pl.pallas_call(
        _qkv_kernel,
        grid=(n_rows // rows,),
        in_specs=[pl.BlockSpec((rows, d), lambda t: (t, 0))] + [_layer_spec(c, layer) for c in stacked]
        + [tab_spec],
        out_specs=[qk_spec, qk_spec, v_spec],
        out_shape=[jax.ShapeDtypeStruct((SUBLANES, MLA_HEADS, t_all, 2 * LANES), BF16),
                   jax.ShapeDtypeStruct((SUBLANES, MLA_HEADS, t_all, 2 * LANES), BF16),
                   jax.ShapeDtypeStruct((SUBLANES, MLA_HEADS, t_all, V_HEAD + LANES), BF16)],
        scratch_shapes=[pltpu.VMEM((n_lat_cols // LANES, rows, LANES), F32)],
        compiler_params=_params(1),
        name="mla_qkv",
    )(h, *stacked, tabs)


def _attn_call(q, k, v):
    b, nh, t_all, dk = q.shape
    return pl.pallas_call(
        _attn_kernel,
        grid=(b, nh // ATTN_HEADS),
        in_specs=[pl.BlockSpec((1, ATTN_HEADS, t_all, dk), lambda bi, hi: (bi, hi, 0, 0)),
                  pl.BlockSpec((1, ATTN_HEADS, t_all, dk), lambda bi, hi: (bi, hi, 0, 0)),
                  pl.BlockSpec((1, ATTN_HEADS, t_all, v.shape[3]), lambda bi, hi: (bi, hi, 0, 0))],
        out_specs=pl.BlockSpec((1, t_all, ATTN_HEADS * V_HEAD), lambda bi, hi: (bi, 0, hi)),
        out_shape=jax.ShapeDtypeStruct((b, t_all, nh * V_HEAD), BF16),
        compiler_params=_params(2),
        name="mla_attention",
    )(q, k, v)


def _rope_tables(t_all):
    inv = ROPE_THETA ** (-jnp.arange(0, QK_ROPE, 2, dtype=F32) / QK_ROPE)
    ang = jnp.arange(t_all, dtype=F32)[:, None] * inv[None, :]
    c, s = jnp.cos(ang), jnp.sin(ang)
    return jnp.concatenate([c, c, s, s, c, c, c, c, -s, s, -s, s], axis=1)


def _rotate_half(a):
    half = a.shape[-1] // 2
    return jnp.concatenate([-a[..., half:], a[..., :half]], axis=-1)


def _swap_halves(a):
    half = a.shape[-1] // 2
    return jnp.concatenate([a[..., half:], a[..., :half]], axis=-1)


def kernel(x, meta, norm_g, w_in, rg_conv_w, rg_conv_b, rg_wa, rg_ba, rg_wx, rg_bx, rg_lambda, rg_out,
           sc_conv_w, sc_out, mla_cq_g, mla_w_uq, mla_ckv_g, mla_w_uk, mla_w_uv, mla_qnorm_g,
           mla_knorm_g, mla_out, gate_b, w_out):
    b, seq, d = x.shape
    depth = norm_g.shape[0]
    d_rnn = rg_out.shape[1]
    d_sc = sc_out.shape[1]
    t_all = N_META + seq
    assert b == SUBLANES, "time-major layout needs the batch to fill one f32 sublane tile"
    assert t_all % STEPS_PER_TILE == 0

    h = _pack_call(x, meta.astype(x.dtype))
    tabs = _rope_tables(t_all)

    o_sc = 2 * d_rnn
    o_cq = o_sc + 4 * d_sc
    o_kr = o_cq + Q_LORA + KV_LORA
    o_ga = o_kr + QK_ROPE
    n_grp = MLA_HEADS // HEAD_GROUP

    row = lambda a: a.reshape(depth, 1, -1)
    w_t = jnp.swapaxes(w_in, 1, 2)
    w_all = w_t.astype(BF16)
    w_kr = w_all[:, o_kr:o_ga]
    w_lat = jnp.concatenate([w_all[:, o_cq:o_kr], w_kr, w_kr], axis=1)

    wq = mla_w_uq.reshape(depth, Q_LORA, MLA_HEADS, QK_HEAD)
    wq_rope = wq[..., QK_NOPE:]
    wq_heads = jnp.concatenate([wq[..., :QK_NOPE], wq_rope, _rotate_half(wq_rope)], axis=3)
    wq_grp = wq_heads.reshape(depth, Q_LORA, n_grp, HEAD_GROUP * 2 * LANES).transpose(0, 2, 1, 3).astype(BF16)
    wk = mla_w_uk.reshape(depth, KV_LORA, MLA_HEADS, QK_NOPE)
    wv = mla_w_uv.reshape(depth, KV_LORA, MLA_HEADS, V_HEAD)
    wkv_grp = jnp.concatenate([wk, wv], axis=3).reshape(
        depth, KV_LORA, n_grp, HEAD_GROUP * 2 * LANES).transpose(0, 2, 1, 3).astype(BF16)
    qg_rope, kg_rope = mla_qnorm_g[:, QK_NOPE:], mla_knorm_g[:, QK_NOPE:]

    ng = row(norm_g)
    qk_const = QK_HEAD ** 0.5 * LOG2_E
    qkv_params = (ng, w_lat, row(mla_cq_g), row(mla_ckv_g), wq_grp, wkv_grp,
                  row(mla_qnorm_g[:, :QK_NOPE] * mla_knorm_g[:, :QK_NOPE] * qk_const),
                  row(jnp.concatenate([qg_rope, _swap_halves(qg_rope)], axis=1) * qk_const),
                  row(jnp.concatenate([kg_rope, kg_rope], axis=1)))
    mixer_params = (ng, rg_conv_w, row(rg_conv_b), jnp.concatenate([rg_wa, rg_wx], axis=-1).astype(BF16),
                    row(rg_ba), row(rg_bx), row(rg_lambda), rg_out.astype(BF16), sc_conv_w, sc_out.astype(BF16),
                    row(gate_b), mla_out.astype(BF16), w_out.astype(BF16))

    for l in range(depth):
        q, k, v = _qkv_call(h, l, qkv_params, tabs)
        o = _attn_call(q, k, v)
        h = _mixer_call(h, o, l, w_all, (o_ga, w_all.shape[1] - o_ga), mixer_params)
    return _unpack_call(h, b, seq)
```

```python
import functools

import jax
import jax.numpy as jnp
from jax import lax
from jax.experimental import pallas as pl
from jax.experimental.pallas import tpu as pltpu

N_META = 16
EPS = 1e-6
RNN_BLOCKS = 4
RNN_CONV = 4
LRU_C = 8.0
SC_CONV = 3
MLA_HEADS = 8
QK_NOPE = 128
QK_ROPE = 64
QK_HEAD = QK_NOPE + QK_ROPE
V_HEAD = 128
Q_LORA = 384
KV_LORA = 256
ROPE_THETA = 10000.0
LOG2_E = 1.4426950408889634

LANES = 128
SUBLANES = 8
STEPS_PER_TILE = 48
ATTN_TILE = 256
ATTN_MACRO = 512
ATTN_HEADS = 2
HEAD_GROUP = 2
W_BLOCK = 2048
PACK_STEPS = 16
VMEM_LIMIT = 56 * 1024 * 1024
MIXER_VMEM_LIMIT = 60 * 1024 * 1024

F32 = jnp.float32
BF16 = jnp.bfloat16


def _rms(x, g):
    ms = jnp.mean(x * x, axis=-1, keepdims=True)
    return x * lax.rsqrt(ms + EPS) * g


def _silu(x):
    return x * jax.nn.sigmoid(x)


def _dot_t(a, w_t):
    return lax.dot_general(a, w_t, (((1,), (1,)), ((), ())), preferred_element_type=F32)


def _causal_taps(buf, cur, taps, halo, rows):
    width = len(taps)
    acc = taps[width - 1] * cur
    for k in range(width - 1):
        back = (width - 1 - k) * SUBLANES
        acc = acc + taps[k] * buf[pl.ds(halo - back, rows), :]
    return acc


def _mixer_kernel(x_ref, o_ref, ng_ref, wr_ref, ws0_ref, ws1_ref, wm_ref,
                  rcw_ref, rcb_ref, wax_ref, ba_ref, bx_ref, lam_ref, wro_ref,
                  scw_ref, wso_ref, gb_ref, wao_ref, wout_ref,
                  out_ref, xbuf, h_buf, h_carry, cbuf, o_stage, *out_scratch, n_meta=None):
    rows, d = x_ref.shape
    steps = o_ref.shape[1]
    d_rnn = wro_ref.shape[0]
    d_sc = wso_ref.shape[0]
    blk = d_rnn // RNN_BLOCKS
    rnn_halo = (RNN_CONV - 1) * SUBLANES
    sc_halo = (SC_CONV - 1) * SUBLANES

    @pl.when(pl.program_id(0) == 0)
    def _():
        xbuf[pl.ds(0, rnn_halo), :] = jnp.zeros((rnn_halo, d_rnn), F32)
        cbuf[pl.ds(0, sc_halo), :] = jnp.zeros((sc_halo, d_sc), F32)
        h_carry[...] = jnp.zeros_like(h_carry)

    for b in range(SUBLANES):
        ob = o_ref[b].astype(F32)
        for j in range(o_stage.shape[0]):
            o_stage[j, pl.ds(b, steps, stride=SUBLANES), :] = ob[:, j * LANES:(j + 1) * LANES]

    x = x_ref[...]
    h = _rms(x, ng_ref[...]).astype(BF16)

    zr = _dot_t(h, wr_ref[...])
    zs0 = _dot_t(h, ws0_ref[...])
    xr = zr[:, :d_rnn]
    xbuf[pl.ds(rnn_halo, rows), :] = xr
    taps = [rcw_ref[pl.ds(k, 1), :] for k in range(RNN_CONV)]
    xc = _causal_taps(xbuf, xr, taps, rnn_halo, rows) + rcb_ref[...]
    xbuf[pl.ds(0, rnn_halo), :] = xbuf[pl.ds(rows, rnn_halo), :]
    xcb = xc.astype(BF16)
    ra, ri = [], []
    for n in range(RNN_BLOCKS):
        xn = xcb[:, n * blk:(n + 1) * blk]
        gate_n = jnp.dot(xn, wax_ref[n], preferred_element_type=F32)
        ra.append(gate_n[:, :blk])
        ri.append(gate_n[:, blk:])

    zs1 = _dot_t(h, ws1_ref[...])

    r = jax.nn.sigmoid(jnp.concatenate(ra, axis=-1) + ba_ref[...])
    i = jax.nn.sigmoid(jnp.concatenate(ri, axis=-1) + bx_ref[...])
    lam = lam_ref[...]
    softplus_neg = jnp.maximum(-lam, 0.0) + jnp.log1p(jnp.exp(-jnp.abs(lam)))
    a = jnp.exp((-LRU_C) * r * softplus_neg)
    u = jnp.sqrt(1.0 - a * a) * (i * xc)

    zm = _dot_t(h, wm_ref[...])

    hc = h_carry[...]
    for t in range(rows // SUBLANES):
        sl = slice(t * SUBLANES, (t + 1) * SUBLANES)
        hc = a[sl] * hc + u[sl]
        h_buf[sl, :] = hc
    h_carry[...] = hc

    cx = zs0[:, d_sc:] * zs1[:, :d_sc]
    cbuf[pl.ds(sc_halo, rows), :] = cx
    staps = [scw_ref[pl.ds(k, 1), :] for k in range(SC_CONV)]
    conv = _causal_taps(cbuf, cx, staps, sc_halo, rows)
    cbuf[pl.ds(0, sc_halo), :] = cbuf[pl.ds(rows, sc_halo), :]
    y_sc = jnp.dot((zs0[:, :d_sc] * conv * _silu(zs1[:, d_sc:])).astype(BF16), wso_ref[...],
                   preferred_element_type=F32)

    y_rnn = jnp.dot((h_buf[...] * _silu(zr[:, d_rnn:])).astype(BF16), wro_ref[...],
                    preferred_element_type=F32)

    o_tm = jnp.concatenate([o_stage[j] for j in range(o_stage.shape[0])], axis=1)
    y_att = jnp.dot((o_tm * _silu(zm[:, :d])).astype(BF16), wao_ref[...], preferred_element_type=F32)

    gates = jax.nn.sigmoid(zm[:, d:] + gb_ref[...])
    merged = gates[:, :d] * y_rnn + gates[:, d:2 * d] * y_sc + gates[:, 2 * d:] * y_att
    res = x + jnp.dot(merged.astype(BF16), wout_ref[...], preferred_element_type=F32)
    if n_meta is None:
        out_ref[...] = res
    else:
        _store_batch_major(res, out_ref, o_stage, *out_scratch, n_meta)


def _store_batch_major(res, out_hbm, stage, out_stage, sems, n_meta):
    t = pl.program_id(0)
    nb, steps, d = out_stage.shape[1:]
    slot = lax.rem(t, 2)

    def tile_copy(slot_, t_):
        first_row = pl.multiple_of(t_ * steps - n_meta, n_meta)
        return pltpu.make_async_copy(out_stage.at[slot_], out_hbm.at[:, pl.ds(first_row, steps), :],
                                     sems.at[slot_])

    @pl.when(t >= 3)
    def _():
        tile_copy(slot, t - 2).wait()

    for j in range(stage.shape[0]):
        stage[j] = res[:, j * LANES:(j + 1) * LANES]
    for b in range(nb):
        out_stage[slot, b] = jnp.concatenate(
            [stage[j, pl.ds(b, steps, stride=nb), :] for j in range(stage.shape[0])], axis=1)

    @pl.when(t == 0)
    def _():
        head = pltpu.make_async_copy(out_stage.at[0, :, pl.ds(n_meta, steps - n_meta), :],
                                     out_hbm.at[:, pl.ds(0, steps - n_meta), :], sems.at[0])
        head.start()
        head.wait()

    @pl.when(t > 0)
    def _():
        tile_copy(slot, t).start()

    @pl.when(t == pl.num_programs(0) - 1)
    def _():
        tile_copy(1 - slot, t - 1).wait()
        tile_copy(slot, t).wait()


def _per_batch_rows(buf, steps):
    return jnp.concatenate(
        [jnp.concatenate([buf[j, pl.ds(b, steps, stride=SUBLANES), :] for j in range(buf.shape[0])], axis=1)
         for b in range(SUBLANES)], axis=0)


def _qkv_kernel(x_ref, ng_ref, w_ref, cqg_ref, ckvg_ref, wq_ref, wkv_ref, qgn_ref, qgr_ref,
                kgr_ref, tab_ref, q_ref, k_ref, v_ref, lat_buf):
    rows = x_ref.shape[0]
    steps = rows // SUBLANES
    n_lat = Q_LORA + KV_LORA
    norm_eps = QK_HEAD * EPS
    ones = jnp.ones((SUBLANES, steps, LANES), v_ref.dtype)

    h = _rms(x_ref[...], ng_ref[...]).astype(BF16)
    z = _dot_t(h, w_ref[...])
    for j in range(lat_buf.shape[0]):
        lat_buf[j] = z[:, j * LANES:(j + 1) * LANES]
    zb = _per_batch_rows(lat_buf, steps)
    kr = zb[:, n_lat:]
    cq = _rms(zb[:, :Q_LORA], cqg_ref[...]).astype(BF16)
    ckv = _rms(zb[:, Q_LORA:n_lat], ckvg_ref[...]).astype(BF16)

    def tiled(j):
        return jnp.concatenate([tab_ref[:, j * LANES:(j + 1) * LANES]] * SUBLANES, axis=0)

    q_tab = tiled(0) * qgr_ref[...]
    kr_half_sq = 0.5 * kr * kr
    krg = kr * kgr_ref[...]
    kr_rot = krg * tiled(1) + pltpu.roll(krg, QK_ROPE // 2, axis=1) * tiled(2)

    def per_batch(a):
        return a.reshape(SUBLANES, steps, a.shape[-1])

    for grp in range(MLA_HEADS // HEAD_GROUP):
        qg = jnp.dot(cq, wq_ref[grp], preferred_element_type=F32)
        kvg = jnp.dot(ckv, wkv_ref[grp], preferred_element_type=F32)
        for j in range(HEAD_GROUP):
            hd = grp * HEAD_GROUP + j
            qn = qg[:, (2 * j) * LANES:(2 * j + 1) * LANES]
            qx = qg[:, (2 * j + 1) * LANES:(2 * j + 2) * LANES]
            q_ss = jnp.sum(qn * qn + 0.5 * (qx * qx), axis=-1, keepdims=True)
            q_inv = lax.rsqrt(q_ss + norm_eps)
            q_ref[:, hd, :, pl.ds(0, QK_NOPE)] = per_batch(qn * q_inv * qgn_ref[...]).astype(q_ref.dtype)
            q_ref[:, hd, :, pl.ds(QK_NOPE, LANES)] = per_batch(qx * q_tab * q_inv).astype(q_ref.dtype)

            kn = kvg[:, (2 * j) * LANES:(2 * j + 1) * LANES]
            k_ss = jnp.sum(kn * kn + kr_half_sq, axis=-1, keepdims=True)
            k_inv = lax.rsqrt(k_ss + norm_eps)
            k_ref[:, hd, :, pl.ds(0, QK_NOPE)] = per_batch(kn * k_inv).astype(k_ref.dtype)
            k_ref[:, hd, :, pl.ds(QK_NOPE, LANES)] = per_batch(kr_rot * k_inv).astype(k_ref.dtype)
            v_ref[:, hd, :, pl.ds(0, V_HEAD)] = per_batch(
                kvg[:, (2 * j + 1) * LANES:(2 * j + 2) * LANES]).astype(v_ref.dtype)
            v_ref[:, hd, :, pl.ds(V_HEAD, LANES)] = ones


def _attn_kernel(q_ref, k_ref, v_ref, o_ref):
    t_all = q_ref.shape[2]
    nt = (((1,), (1,)), ((), ()))

    def scores(hd, start, size):
        def keys(lo, n):
            return k_ref[0, hd, pl.ds(lo, n), :]

        q = q_ref[0, hd, pl.ds(start, size), :]
        s_off = lax.dot_general(q, keys(0, start), nt, preferred_element_type=F32) if start else None
        sub = min(size, ATTN_TILE)
        row = lax.broadcasted_iota(jnp.int32, (sub, sub), 0)
        col = lax.broadcasted_iota(jnp.int32, (sub, sub), 1)
        near = []
        for r in range(0, size, sub):
            qr = q[r:r + sub]
            s_in = lax.dot_general(qr, keys(start, r), nt, preferred_element_type=F32) if r else None
            s_dg = lax.dot_general(qr, keys(start + r, sub), nt, preferred_element_type=F32)
            near.append((s_in, jnp.where(col <= row, s_dg, -1e30)))
        return s_off, near

    def finish(hd, start, size, s_off, near):
        def vals(lo, n):
            return v_ref[0, hd, pl.ds(lo, n), :]

        sub = min(size, ATTN_TILE)
        maxes, p_offs = [], []
        for n, (s_in, s_dg) in enumerate(near):
            m = jnp.max(s_dg, axis=-1, keepdims=True)
            if s_in is not None:
                m = jnp.maximum(m, jnp.max(s_in, axis=-1, keepdims=True))
            if s_off is not None:
                so = s_off[n * sub:(n + 1) * sub]
                m = jnp.maximum(m, jnp.max(so, axis=-1, keepdims=True))
                p_offs.append(jnp.exp2(so - m).astype(BF16))
            maxes.append(m)
        if s_off is not None:
            p_off = jnp.concatenate(p_offs, axis=0) if len(p_offs) > 1 else p_offs[0]
            acc_off = jnp.dot(p_off, vals(0, start), preferred_element_type=F32)
        for n, (s_in, s_dg) in enumerate(near):
            r = n * sub
            m = maxes[n]
            acc = jnp.dot(jnp.exp2(s_dg - m).astype(BF16), vals(start + r, sub), preferred_element_type=F32)
            if s_in is not None:
                acc = acc + jnp.dot(jnp.exp2(s_in - m).astype(BF16), vals(start, r),
                                    preferred_element_type=F32)
            if s_off is not None:
                acc = acc + acc_off[r:r + sub]
            o_ref[0, pl.ds(start + r, sub), pl.ds(hd * V_HEAD, V_HEAD)] = (
                acc[:, :V_HEAD] / acc[:, V_HEAD:]).astype(o_ref.dtype)

    tiles = [(s, ATTN_MACRO) for s in range(0, t_all - ATTN_MACRO + 1, ATTN_MACRO)]
    done = len(tiles) * ATTN_MACRO
    if done < t_all:
        tiles.append((done, t_all - done))
    heads = range(q_ref.shape[1])
    pending = [scores(hd, *tiles[0]) for hd in heads]
    for idx, (start, size) in enumerate(tiles):
        nxt = [scores(hd, *tiles[idx + 1]) for hd in heads] if idx + 1 < len(tiles) else None
        for hd in heads:
            finish(hd, start, size, *pending[hd])
        pending = nxt


def _pack_kernel(xa_ref, xb_ref, xc_ref, meta_ref, out_ref, stage):
    nb, steps, d = xa_ref.shape
    for part, x_ref in enumerate((xa_ref, xb_ref, xc_ref)):
        base = part * steps * nb
        for b in range(nb):
            xb = x_ref[b]
            for j in range(stage.shape[0]):
                stage[j, pl.ds(base + b, steps, stride=nb), :] = xb[:, j * LANES:(j + 1) * LANES]
    out_ref[...] = jnp.concatenate([stage[j] for j in range(stage.shape[0])], axis=1)

    @pl.when(pl.program_id(0) == 0)
    def _():
        for t in range(steps):
            out_ref[pl.ds(t * nb, nb), :] = jnp.broadcast_to(meta_ref[pl.ds(t, 1), :], (nb, d))


def _const_spec(shape):
    nd = len(shape)
    return pl.BlockSpec(shape, lambda *_: (0,) * nd)


def _layer_spec(arr, layer):
    nd = arr.ndim - 1
    return pl.BlockSpec((None,) + arr.shape[1:], lambda *_: (layer,) + (0,) * nd)


def _pack_call(x, meta):
    b, seq, d = x.shape
    steps = PACK_STEPS
    assert meta.shape[0] == steps and STEPS_PER_TILE == 3 * steps and seq % steps == 0
    n_tiles = (seq + steps) // STEPS_PER_TILE
    rows = _tile_rows()

    def x_block(shift):
        return pl.BlockSpec((b, steps, d), lambda t: (0, jnp.maximum(3 * t + shift, 0), 0))

    return pl.pallas_call(
        _pack_kernel,
        grid=(n_tiles,),
        in_specs=[x_block(-1), x_block(0), x_block(1), _const_spec(meta.shape)],
        out_specs=pl.BlockSpec((rows, d), lambda t: (t, 0)),
        out_shape=jax.ShapeDtypeStruct((n_tiles * rows, d), x.dtype),
        scratch_shapes=[pltpu.VMEM((d // LANES, rows, LANES), F32)],
        compiler_params=_params(1),
        name="pack_time_major",
    )(x, x, x, meta)


def _params(n_axes, vmem_limit=VMEM_LIMIT):
    return pltpu.CompilerParams(dimension_semantics=("arbitrary",) * n_axes, vmem_limit_bytes=vmem_limit)


def _tile_rows():
    return STEPS_PER_TILE * SUBLANES


def _mixer_call(h, o, layer, w_all, gate_rows, stacked, final_seq=None):
    n_rows, d = h.shape
    ng, rcw, rcb, wax, ba, bx, lam, wro, scw, wso, gb, wao, wout = stacked
    d_rnn = wro.shape[1]
    d_sc = wso.shape[1]
    assert 2 * d_rnn == W_BLOCK and 2 * d_sc == W_BLOCK, "rnn | conv column groups must be whole W_BLOCKs"
    rows = _tile_rows()
    row_spec = pl.BlockSpec((rows, d), lambda t: (t, 0))
    o_spec = pl.BlockSpec((SUBLANES, STEPS_PER_TILE, d), lambda t: (0, t, 0))

    def w_col_block(j):
        return pl.BlockSpec((None, W_BLOCK, w_all.shape[2]), lambda t: (layer, j, 0))

    gate_lo, gate_n = gate_rows
    wm_spec = pl.BlockSpec((pl.Squeezed(), pl.Element(gate_n), pl.Element(w_all.shape[2])),
                           lambda t: (layer, gate_lo, 0))
    tail = (rcw, rcb, wax, ba, bx, lam, wro, scw, wso, gb, wao, wout)
    if final_seq is None:
        kern, out_spec, out_scratch = _mixer_kernel, row_spec, []
        out_shape = jax.ShapeDtypeStruct((n_rows, d), F32)
    else:
        n_meta = n_rows // SUBLANES - final_seq
        assert 0 < n_meta < STEPS_PER_TILE and n_meta % SUBLANES == 0 and n_rows // rows > 3
        kern = functools.partial(_mixer_kernel, n_meta=n_meta)
        out_spec = pl.BlockSpec(memory_space=pl.ANY)
        out_shape = jax.ShapeDtypeStruct((SUBLANES, final_seq, d), F32)
        out_scratch = [pltpu.VMEM((2, SUBLANES, STEPS_PER_TILE, d), F32), pltpu.SemaphoreType.DMA((2,))]
    return pl.pallas_call(
        kern,
        grid=(n_rows // rows,),
        in_specs=[row_spec, o_spec, _layer_spec(ng, layer), w_col_block(0), w_col_block(1), w_col_block(2),
                  wm_spec] + [_layer_spec(c, layer) for c in tail],
        out_specs=out_spec,
        out_shape=out_shape,
        scratch_shapes=[pltpu.VMEM((rows + (RNN_CONV - 1) * SUBLANES, d_rnn), F32),
                        pltpu.VMEM((rows, d_rnn), F32), pltpu.VMEM((SUBLANES, d_rnn), F32),
                        pltpu.VMEM((rows + (SC_CONV - 1) * SUBLANES, d_sc), F32),
                        pltpu.VMEM((d // LANES, rows, LANES), F32)] + out_scratch,
        compiler_params=_params(1, MIXER_VMEM_LIMIT),
        name="mixers_merge",
    )(h, o, ng, w_all, w_all, w_all, w_all, *tail)


def _qkv_call(h, layer, stacked, tabs):
    n_rows, d = h.shape
    rows = _tile_rows()
    t_all = n_rows // SUBLANES
    n_lat_cols = stacked[1].shape[1]
    tab_spec = pl.BlockSpec((STEPS_PER_TILE, tabs.shape[1]), lambda t: (t, 0))
    qk_spec = pl.BlockSpec((SUBLANES, MLA_HEADS, STEPS_PER_TILE, 2 * LANES), lambda t: (0, 0, t, 0))
    v_spec = pl.BlockSpec((SUBLANES, MLA_HEADS, STEPS_PER_TILE, V_HEAD + LANES), lambda t: (0, 0, t, 0))
    return pl.pallas_call(
        _qkv_kernel,
        grid=(n_rows // rows,),
        in_specs=[pl.BlockSpec((rows, d), lambda t: (t, 0))] + [_layer_spec(c, layer) for c in stacked]
        + [tab_spec],
        out_specs=[qk_spec, qk_spec, v_spec],
        out_shape=[jax.ShapeDtypeStruct((SUBLANES, MLA_HEADS, t_all, 2 * LANES), BF16),
                   jax.ShapeDtypeStruct((SUBLANES, MLA_HEADS, t_all, 2 * LANES), BF16),
                   jax.ShapeDtypeStruct((SUBLANES, MLA_HEADS, t_all, V_HEAD + LANES), BF16)],
        scratch_shapes=[pltpu.VMEM((n_lat_cols // LANES, rows, LANES), F32)],
        compiler_params=_params(1),
        name="mla_qkv",
    )(h, *stacked, tabs)


def _attn_call(q, k, v):
    b, nh, t_all, dk = q.shape
    return pl.pallas_call(
        _attn_kernel,
        grid=(b, nh // ATTN_HEADS),
        in_specs=[pl.BlockSpec((1, ATTN_HEADS, t_all, dk), lambda bi, hi: (bi, hi, 0, 0)),
                  pl.BlockSpec((1, ATTN_HEADS, t_all, dk), lambda bi, hi: (bi, hi, 0, 0)),
                  pl.BlockSpec((1, ATTN_HEADS, t_all, v.shape[3]), lambda bi, hi: (bi, hi, 0, 0))],
        out_specs=pl.BlockSpec((1, t_all, ATTN_HEADS * V_HEAD), lambda bi, hi: (bi, 0, hi)),
        out_shape=jax.ShapeDtypeStruct((b, t_all, nh * V_HEAD), BF16),
        compiler_params=_params(2),
        name="mla_attention",
    )(q, k, v)


def _rope_tables(t_all):
    inv = ROPE_THETA ** (-jnp.arange(0, QK_ROPE, 2, dtype=F32) / QK_ROPE)
    ang = jnp.arange(t_all, dtype=F32)[:, None] * inv[None, :]
    c, s = jnp.cos(ang), jnp.sin(ang)
    return jnp.concatenate([c, c, s, s, c, c, c, c, -s, s, -s, s], axis=1)


def _rotate_half(a):
    half = a.shape[-1] // 2
    return jnp.concatenate([-a[..., half:], a[..., :half]], axis=-1)


def _swap_halves(a):
    half = a.shape[-1] // 2
    return jnp.concatenate([a[..., half:], a[..., :half]], axis=-1)


def kernel(x, meta, norm_g, w_in, rg_conv_w, rg_conv_b, rg_wa, rg_ba, rg_wx, rg_bx, rg_lambda, rg_out,
           sc_conv_w, sc_out, mla_cq_g, mla_w_uq, mla_ckv_g, mla_w_uk, mla_w_uv, mla_qnorm_g,
           mla_knorm_g, mla_out, gate_b, w_out):
    b, seq, d = x.shape
    depth = norm_g.shape[0]
    d_rnn = rg_out.shape[1]
    d_sc = sc_out.shape[1]
    t_all = N_META + seq
    assert b == SUBLANES, "time-major layout needs the batch to fill one f32 sublane tile"
    assert t_all % STEPS_PER_TILE == 0

    h = _pack_call(x, meta.astype(x.dtype))
    tabs = _rope_tables(t_all)

    o_sc = 2 * d_rnn
    o_cq = o_sc + 4 * d_sc
    o_kr = o_cq + Q_LORA + KV_LORA
    o_ga = o_kr + QK_ROPE
    n_grp = MLA_HEADS // HEAD_GROUP

    row = lambda a: a.reshape(depth, 1, -1)
    w_t = jnp.swapaxes(w_in, 1, 2)
    w_all = w_t.astype(BF16)
    w_kr = w_all[:, o_kr:o_ga]
    w_lat = jnp.concatenate([w_all[:, o_cq:o_kr], w_kr, w_kr], axis=1)

    wq = mla_w_uq.reshape(depth, Q_LORA, MLA_HEADS, QK_HEAD)
    wq_rope = wq[..., QK_NOPE:]
    wq_heads = jnp.concatenate([wq[..., :QK_NOPE], wq_rope, _rotate_half(wq_rope)], axis=3)
    wq_grp = wq_heads.reshape(depth, Q_LORA, n_grp, HEAD_GROUP * 2 * LANES).transpose(0, 2, 1, 3).astype(BF16)
    wk = mla_w_uk.reshape(depth, KV_LORA, MLA_HEADS, QK_NOPE)
    wv = mla_w_uv.reshape(depth, KV_LORA, MLA_HEADS, V_HEAD)
    wkv_grp = jnp.concatenate([wk, wv], axis=3).reshape(
        depth, KV_LORA, n_grp, HEAD_GROUP * 2 * LANES).transpose(0, 2, 1, 3).astype(BF16)
    qg_rope, kg_rope = mla_qnorm_g[:, QK_NOPE:], mla_knorm_g[:, QK_NOPE:]

    ng = row(norm_g)
    qk_const = QK_HEAD ** 0.5 * LOG2_E
    qkv_params = (ng, w_lat, row(mla_cq_g), row(mla_ckv_g), wq_grp, wkv_grp,
                  row(mla_qnorm_g[:, :QK_NOPE] * mla_knorm_g[:, :QK_NOPE] * qk_const),
                  row(jnp.concatenate([qg_rope, _swap_halves(qg_rope)], axis=1) * qk_const),
                  row(jnp.concatenate([kg_rope, kg_rope], axis=1)))
    mixer_params = (ng, rg_conv_w, row(rg_conv_b), jnp.concatenate([rg_wa, rg_wx], axis=-1).astype(BF16),
                    row(rg_ba), row(rg_bx), row(rg_lambda), rg_out.astype(BF16), sc_conv_w, sc_out.astype(BF16),
                    row(gate_b), mla_out.astype(BF16), w_out.astype(BF16))

    for l in range(depth):
        q, k, v = _qkv_call(h, l, qkv_params, tabs)
        o = _attn_call(q, k, v)
        h = _mixer_call(h, o, l, w_all, (o_ga, w_all.shape[1] - o_ga), mixer_params,
                        final_seq=seq if l == depth - 1 else None)
    return h
```

```python
import functools

import jax
import jax.numpy as jnp
from jax import lax
from jax.experimental import pallas as pl
from jax.experimental.pallas import tpu as pltpu

N_META = 16
EPS = 1e-6
RNN_BLOCKS = 4
RNN_CONV = 4
LRU_C = 8.0
SC_CONV = 3
MLA_HEADS = 8
QK_NOPE = 128
QK_ROPE = 64
QK_HEAD = QK_NOPE + QK_ROPE
V_HEAD = 128
Q_LORA = 384
KV_LORA = 256
ROPE_THETA = 10000.0
LOG2_E = 1.4426950408889634

LANES = 128
SUBLANES = 8
STEPS_PER_TILE = 48
ATTN_TILE = 256
ATTN_MACRO = 512
ATTN_HEADS = 2
HEAD_GROUP = 2
W_BLOCK = 2048
VMEM_LIMIT = 56 * 1024 * 1024
MIXER_VMEM_LIMIT = 60 * 1024 * 1024

F32 = jnp.float32
BF16 = jnp.bfloat16


def _rms(x, g):
    ms = jnp.mean(x * x, axis=-1, keepdims=True)
    return x * lax.rsqrt(ms + EPS) * g


def _silu(x):
    return x * jax.nn.sigmoid(x)


def _dot_t(a, w_t):
    return lax.dot_general(a, w_t, (((1,), (1,)), ((), ())), preferred_element_type=F32)


def _causal_taps(buf, cur, taps, halo, rows):
    width = len(taps)
    acc = taps[width - 1] * cur
    for k in range(width - 1):
        back = (width - 1 - k) * SUBLANES
        acc = acc + taps[k] * buf[pl.ds(halo - back, rows), :]
    return acc


def _mixer_kernel(x_ref, o_ref, ng_ref, wr_ref, ws0_ref, ws1_ref, wm_ref,
                  rcw_ref, rcb_ref, wax_ref, ba_ref, bx_ref, lam_ref, wro_ref,
                  scw_ref, wso_ref, gb_ref, wao_ref, wout_ref,
                  out_ref, xbuf, h_buf, h_carry, cbuf, o_stage, *out_scratch, n_meta=None):
    rows, d = x_ref.shape
    steps = o_ref.shape[1]
    d_rnn = wro_ref.shape[0]
    d_sc = wso_ref.shape[0]
    blk = d_rnn // RNN_BLOCKS
    rnn_halo = (RNN_CONV - 1) * SUBLANES
    sc_halo = (SC_CONV - 1) * SUBLANES

    @pl.when(pl.program_id(0) == 0)
    def _():
        xbuf[pl.ds(0, rnn_halo), :] = jnp.zeros((rnn_halo, d_rnn), F32)
        cbuf[pl.ds(0, sc_halo), :] = jnp.zeros((sc_halo, d_sc), F32)
        h_carry[...] = jnp.zeros_like(h_carry)

    for b in range(SUBLANES):
        ob = o_ref[b].astype(F32)
        for j in range(o_stage.shape[0]):
            o_stage[j, pl.ds(b, steps, stride=SUBLANES), :] = ob[:, j * LANES:(j + 1) * LANES]

    x = x_ref[...]
    h = _rms(x, ng_ref[...]).astype(BF16)

    zr = _dot_t(h, wr_ref[...])
    zs0 = _dot_t(h, ws0_ref[...])
    xr = zr[:, :d_rnn]
    xbuf[pl.ds(rnn_halo, rows), :] = xr
    taps = [rcw_ref[pl.ds(k, 1), :] for k in range(RNN_CONV)]
    xc = _causal_taps(xbuf, xr, taps, rnn_halo, rows) + rcb_ref[...]
    xbuf[pl.ds(0, rnn_halo), :] = xbuf[pl.ds(rows, rnn_halo), :]
    xcb = xc.astype(BF16)
    ra, ri = [], []
    for n in range(RNN_BLOCKS):
        xn = xcb[:, n * blk:(n + 1) * blk]
        gate_n = jnp.dot(xn, wax_ref[n], preferred_element_type=F32)
        ra.append(gate_n[:, :blk])
        ri.append(gate_n[:, blk:])

    zs1 = _dot_t(h, ws1_ref[...])

    r = jax.nn.sigmoid(jnp.concatenate(ra, axis=-1) + ba_ref[...])
    i = jax.nn.sigmoid(jnp.concatenate(ri, axis=-1) + bx_ref[...])
    lam = lam_ref[...]
    softplus_neg = jnp.maximum(-lam, 0.0) + jnp.log1p(jnp.exp(-jnp.abs(lam)))
    a = jnp.exp((-LRU_C) * r * softplus_neg)
    u = jnp.sqrt(1.0 - a * a) * (i * xc)

    zm = _dot_t(h, wm_ref[...])

    hc = h_carry[...]
    for t in range(rows // SUBLANES):
        sl = slice(t * SUBLANES, (t + 1) * SUBLANES)
        hc = a[sl] * hc + u[sl]
        h_buf[sl, :] = hc
    h_carry[...] = hc

    cx = zs0[:, d_sc:] * zs1[:, :d_sc]
    cbuf[pl.ds(sc_halo, rows), :] = cx
    staps = [scw_ref[pl.ds(k, 1), :] for k in range(SC_CONV)]
    conv = _causal_taps(cbuf, cx, staps, sc_halo, rows)
    cbuf[pl.ds(0, sc_halo), :] = cbuf[pl.ds(rows, sc_halo), :]
    y_sc = jnp.dot((zs0[:, :d_sc] * conv * _silu(zs1[:, d_sc:])).astype(BF16), wso_ref[...],
                   preferred_element_type=F32)

    y_rnn = jnp.dot((h_buf[...] * _silu(zr[:, d_rnn:])).astype(BF16), wro_ref[...],
                    preferred_element_type=F32)

    o_tm = jnp.concatenate([o_stage[j] for j in range(o_stage.shape[0])], axis=1)
    y_att = jnp.dot((o_tm * _silu(zm[:, :d])).astype(BF16), wao_ref[...], preferred_element_type=F32)

    gates = jax.nn.sigmoid(zm[:, d:] + gb_ref[...])
    merged = gates[:, :d] * y_rnn + gates[:, d:2 * d] * y_sc + gates[:, 2 * d:] * y_att
    res = x + jnp.dot(merged.astype(BF16), wout_ref[...], preferred_element_type=F32)
    if n_meta is None:
        out_ref[...] = res
    else:
        _store_batch_major(res, out_ref, o_stage, *out_scratch, n_meta)


def _store_batch_major(res, out_hbm, stage, out_stage, sems, n_meta):
    t = pl.program_id(0)
    nb, steps, d = out_stage.shape[1:]
    slot = lax.rem(t, 2)

    def tile_copy(slot_, t_):
        first_row = pl.multiple_of(t_ * steps - n_meta, n_meta)
        return pltpu.make_async_copy(out_stage.at[slot_], out_hbm.at[:, pl.ds(first_row, steps), :],
                                     sems.at[slot_])

    @pl.when(t >= 3)
    def _():
        tile_copy(slot, t - 2).wait()

    for j in range(stage.shape[0]):
        stage[j] = res[:, j * LANES:(j + 1) * LANES]
    for b in range(nb):
        out_stage[slot, b] = jnp.concatenate(
            [stage[j, pl.ds(b, steps, stride=nb), :] for j in range(stage.shape[0])], axis=1)

    @pl.when(t == 0)
    def _():
        head = pltpu.make_async_copy(out_stage.at[0, :, pl.ds(n_meta, steps - n_meta), :],
                                     out_hbm.at[:, pl.ds(0, steps - n_meta), :], sems.at[0])
        head.start()
        head.wait()

    @pl.when(t > 0)
    def _():
        tile_copy(slot, t).start()

    @pl.when(t == pl.num_programs(0) - 1)
    def _():
        tile_copy(1 - slot, t - 1).wait()
        tile_copy(slot, t).wait()


def _per_batch_rows(buf, steps):
    return jnp.concatenate(
        [jnp.concatenate([buf[j, pl.ds(b, steps, stride=SUBLANES), :] for j in range(buf.shape[0])], axis=1)
         for b in range(SUBLANES)], axis=0)


def _qkv_kernel(x_ref, ng_ref, w_ref, cqg_ref, ckvg_ref, wq_ref, wkv_ref, qgn_ref, qgr_ref,
                kgr_ref, tab_ref, q_ref, k_ref, v_ref, lat_buf):
    steps = x_ref.shape[0] // SUBLANES
    h = _rms(x_ref[...], ng_ref[...]).astype(BF16)
    z = _dot_t(h, w_ref[...])
    for j in range(lat_buf.shape[0]):
        lat_buf[j] = z[:, j * LANES:(j + 1) * LANES]
    _qkv_heads(_per_batch_rows(lat_buf, steps), steps, cqg_ref, ckvg_ref, wq_ref, wkv_ref, qgn_ref, qgr_ref,
               kgr_ref, tab_ref, q_ref, k_ref, v_ref)


def _qkv_first_kernel(xa_ref, xb_ref, xc_ref, meta_ref, ng_ref, w_ref, cqg_ref, ckvg_ref, wq_ref, wkv_ref,
                      qgn_ref, qgr_ref, kgr_ref, tab_ref, q_ref, k_ref, v_ref, h0_ref, stage):
    nb, part_steps, d = xa_ref.shape
    steps = 3 * part_steps
    first = pl.program_id(0) == 0
    per_batch = []
    for b in range(nb):
        xa = jnp.where(first, meta_ref[...], xa_ref[b])
        xb = jnp.concatenate([xa, xb_ref[b], xc_ref[b]], axis=0)
        for j in range(stage.shape[0]):
            stage[j, pl.ds(b, steps, stride=nb), :] = xb[:, j * LANES:(j + 1) * LANES]
        per_batch.append(xb)
    h0_ref[...] = jnp.concatenate([stage[j] for j in range(stage.shape[0])], axis=1)
    h = _rms(jnp.concatenate(per_batch, axis=0), ng_ref[...]).astype(BF16)
    _qkv_heads(_dot_t(h, w_ref[...]), steps, cqg_ref, ckvg_ref, wq_ref, wkv_ref, qgn_ref, qgr_ref,
               kgr_ref, tab_ref, q_ref, k_ref, v_ref)


def _qkv_heads(zb, steps, cqg_ref, ckvg_ref, wq_ref, wkv_ref, qgn_ref, qgr_ref, kgr_ref, tab_ref,
               q_ref, k_ref, v_ref):
    n_lat = Q_LORA + KV_LORA
    norm_eps = QK_HEAD * EPS
    ones = jnp.ones((SUBLANES, steps, LANES), v_ref.dtype)
    kr = zb[:, n_lat:]
    cq = _rms(zb[:, :Q_LORA], cqg_ref[...]).astype(BF16)
    ckv = _rms(zb[:, Q_LORA:n_lat], ckvg_ref[...]).astype(BF16)

    def tiled(j):
        return jnp.concatenate([tab_ref[:, j * LANES:(j + 1) * LANES]] * SUBLANES, axis=0)

    q_tab = tiled(0) * qgr_ref[...]
    kr_half_sq = 0.5 * kr * kr
    krg = kr * kgr_ref[...]
    kr_rot = krg * tiled(1) + pltpu.roll(krg, QK_ROPE // 2, axis=1) * tiled(2)

    def per_batch(a):
        return a.reshape(SUBLANES, steps, a.shape[-1])

    for grp in range(MLA_HEADS // HEAD_GROUP):
        qg = jnp.dot(cq, wq_ref[grp], preferred_element_type=F32)
        kvg = jnp.dot(ckv, wkv_ref[grp], preferred_element_type=F32)
        for j in range(HEAD_GROUP):
            hd = grp * HEAD_GROUP + j
            qn = qg[:, (2 * j) * LANES:(2 * j + 1) * LANES]
            qx = qg[:, (2 * j + 1) * LANES:(2 * j + 2) * LANES]
            q_ss = jnp.sum(qn * qn + 0.5 * (qx * qx), axis=-1, keepdims=True)
            q_inv = lax.rsqrt(q_ss + norm_eps)
            q_ref[:, hd, :, pl.ds(0, QK_NOPE)] = per_batch(qn * q_inv * qgn_ref[...]).astype(q_ref.dtype)
            q_ref[:, hd, :, pl.ds(QK_NOPE, LANES)] = per_batch(qx * q_tab * q_inv).astype(q_ref.dtype)

            kn = kvg[:, (2 * j) * LANES:(2 * j + 1) * LANES]
            k_ss = jnp.sum(kn * kn + kr_half_sq, axis=-1, keepdims=True)
            k_inv = lax.rsqrt(k_ss + norm_eps)
            k_ref[:, hd, :, pl.ds(0, QK_NOPE)] = per_batch(kn * k_inv).astype(k_ref.dtype)
            k_ref[:, hd, :, pl.ds(QK_NOPE, LANES)] = per_batch(kr_rot * k_inv).astype(k_ref.dtype)
            v_ref[:, hd, :, pl.ds(0, V_HEAD)] = per_batch(
                kvg[:, (2 * j + 1) * LANES:(2 * j + 2) * LANES]).astype(v_ref.dtype)
            v_ref[:, hd, :, pl.ds(V_HEAD, LANES)] = ones


def _attn_kernel(q_ref, k_ref, v_ref, o_ref):
    t_all = q_ref.shape[2]
    nt = (((1,), (1,)), ((), ()))

    def scores(hd, start, size):
        def keys(lo, n):
            return k_ref[0, hd, pl.ds(lo, n), :]

        q = q_ref[0, hd, pl.ds(start, size), :]
        s_off = lax.dot_general(q, keys(0, start), nt, preferred_element_type=F32) if start else None
        sub = min(size, ATTN_TILE)
        row = lax.broadcasted_iota(jnp.int32, (sub, sub), 0)
        col = lax.broadcasted_iota(jnp.int32, (sub, sub), 1)
        near = []
        for r in range(0, size, sub):
            qr = q[r:r + sub]
            s_in = lax.dot_general(qr, keys(start, r), nt, preferred_element_type=F32) if r else None
            s_dg = lax.dot_general(qr, keys(start + r, sub), nt, preferred_element_type=F32)
            near.append((s_in, jnp.where(col <= row, s_dg, -1e30)))
        return s_off, near

    def finish(hd, start, size, s_off, near):
        def vals(lo, n):
            return v_ref[0, hd, pl.ds(lo, n), :]

        sub = min(size, ATTN_TILE)
        maxes, p_offs = [], []
        for n, (s_in, s_dg) in enumerate(near):
            m = jnp.max(s_dg, axis=-1, keepdims=True)
            if s_in is not None:
                m = jnp.maximum(m, jnp.max(s_in, axis=-1, keepdims=True))
            if s_off is not None:
                so = s_off[n * sub:(n + 1) * sub]
                m = jnp.maximum(m, jnp.max(so, axis=-1, keepdims=True))
                p_offs.append(jnp.exp2(so - m).astype(BF16))
            maxes.append(m)
        if s_off is not None:
            p_off = jnp.concatenate(p_offs, axis=0) if len(p_offs) > 1 else p_offs[0]
            acc_off = jnp.dot(p_off, vals(0, start), preferred_element_type=F32)
        for n, (s_in, s_dg) in enumerate(near):
            r = n * sub
            m = maxes[n]
            acc = jnp.dot(jnp.exp2(s_dg - m).astype(BF16), vals(start + r, sub), preferred_element_type=F32)
            if s_in is not None:
                acc = acc + jnp.dot(jnp.exp2(s_in - m).astype(BF16), vals(start, r),
                                    preferred_element_type=F32)
            if s_off is not None:
                acc = acc + acc_off[r:r + sub]
            o_ref[0, pl.ds(start + r, sub), pl.ds(hd * V_HEAD, V_HEAD)] = (
                acc[:, :V_HEAD] / acc[:, V_HEAD:]).astype(o_ref.dtype)

    tiles = [(s, ATTN_MACRO) for s in range(0, t_all - ATTN_MACRO + 1, ATTN_MACRO)]
    done = len(tiles) * ATTN_MACRO
    if done < t_all:
        tiles.append((done, t_all - done))
    heads = range(q_ref.shape[1])
    pending = [scores(hd, *tiles[0]) for hd in heads]
    for idx, (start, size) in enumerate(tiles):
        nxt = [scores(hd, *tiles[idx + 1]) for hd in heads] if idx + 1 < len(tiles) else None
        for hd in heads:
            finish(hd, start, size, *pending[hd])
        pending = nxt


def _const_spec(shape):
    nd = len(shape)
    return pl.BlockSpec(shape, lambda *_: (0,) * nd)


def _layer_spec(arr, layer):
    nd = arr.ndim - 1
    return pl.BlockSpec((None,) + arr.shape[1:], lambda *_: (layer,) + (0,) * nd)


def _params(n_axes, vmem_limit=VMEM_LIMIT):
    return pltpu.CompilerParams(dimension_semantics=("arbitrary",) * n_axes, vmem_limit_bytes=vmem_limit)


def _tile_rows():
    return STEPS_PER_TILE * SUBLANES


def _mixer_call(h, o, layer, w_all, gate_rows, stacked, final_seq=None):
    n_rows, d = h.shape
    ng, rcw, rcb, wax, ba, bx, lam, wro, scw, wso, gb, wao, wout = stacked
    d_rnn = wro.shape[1]
    d_sc = wso.shape[1]
    assert 2 * d_rnn == W_BLOCK and 2 * d_sc == W_BLOCK, "rnn | conv column groups must be whole W_BLOCKs"
    rows = _tile_rows()
    row_spec = pl.BlockSpec((rows, d), lambda t: (t, 0))
    o_spec = pl.BlockSpec((SUBLANES, STEPS_PER_TILE, d), lambda t: (0, t, 0))

    def w_col_block(j):
        return pl.BlockSpec((None, W_BLOCK, w_all.shape[2]), lambda t: (layer, j, 0))

    gate_lo, gate_n = gate_rows
    wm_spec = pl.BlockSpec((pl.Squeezed(), pl.Element(gate_n), pl.Element(w_all.shape[2])),
                           lambda t: (layer, gate_lo, 0))
    tail = (rcw, rcb, wax, ba, bx, lam, wro, scw, wso, gb, wao, wout)
    if final_seq is None:
        kern, out_spec, out_scratch = _mixer_kernel, row_spec, []
        out_shape = jax.ShapeDtypeStruct((n_rows, d), F32)
    else:
        n_meta = n_rows // SUBLANES - final_seq
        assert 0 < n_meta < STEPS_PER_TILE and n_meta % SUBLANES == 0 and n_rows // rows > 3
        kern = functools.partial(_mixer_kernel, n_meta=n_meta)
        out_spec = pl.BlockSpec(memory_space=pl.ANY)
        out_shape = jax.ShapeDtypeStruct((SUBLANES, final_seq, d), F32)
        out_scratch = [pltpu.VMEM((2, SUBLANES, STEPS_PER_TILE, d), F32), pltpu.SemaphoreType.DMA((2,))]
    return pl.pallas_call(
        kern,
        grid=(n_rows // rows,),
        in_specs=[row_spec, o_spec, _layer_spec(ng, layer), w_col_block(0), w_col_block(1), w_col_block(2),
                  wm_spec] + [_layer_spec(c, layer) for c in tail],
        out_specs=out_spec,
        out_shape=out_shape,
        scratch_shapes=[pltpu.VMEM((rows + (RNN_CONV - 1) * SUBLANES, d_rnn), F32),
                        pltpu.VMEM((rows, d_rnn), F32), pltpu.VMEM((SUBLANES, d_rnn), F32),
                        pltpu.VMEM((rows + (SC_CONV - 1) * SUBLANES, d_sc), F32),
                        pltpu.VMEM((d // LANES, rows, LANES), F32)] + out_scratch,
        compiler_params=_params(1, MIXER_VMEM_LIMIT),
        name="mixers_merge",
    )(h, o, ng, w_all, w_all, w_all, w_all, *tail)


def _qkv_call(src, layer, stacked, tabs):
    rows = _tile_rows()
    first = isinstance(src, tuple)
    if first:
        x, meta = src
        nb, seq, d = x.shape
        part = STEPS_PER_TILE // 3
        assert meta.shape[0] == part and STEPS_PER_TILE == 3 * part and seq % part == 0 and nb == SUBLANES
        t_all = seq + part
        n_rows = t_all * nb

        def x_block(shift):
            return pl.BlockSpec((nb, part, d), lambda t: (0, jnp.maximum(3 * t + shift, 0), 0))

        kern = _qkv_first_kernel
        data_specs = [x_block(-1), x_block(0), x_block(1), _const_spec(meta.shape)]
        data = (x, x, x, meta)
        stage_tiles = d // LANES
    else:
        n_rows, d = src.shape
        t_all = n_rows // SUBLANES
        kern = _qkv_kernel
        data_specs = [pl.BlockSpec((rows, d), lambda t: (t, 0))]
        data = (src,)
        stage_tiles = stacked[1].shape[1] // LANES
    tab_spec = pl.BlockSpec((STEPS_PER_TILE, tabs.shape[1]), lambda t: (t, 0))
    qk_spec = pl.BlockSpec((SUBLANES, MLA_HEADS, STEPS_PER_TILE, 2 * LANES), lambda t: (0, 0, t, 0))
    v_spec = pl.BlockSpec((SUBLANES, MLA_HEADS, STEPS_PER_TILE, V_HEAD + LANES), lambda t: (0, 0, t, 0))
    out_specs = [qk_spec, qk_spec, v_spec]
    out_shape = [jax.ShapeDtypeStruct((SUBLANES, MLA_HEADS, t_all, 2 * LANES), BF16),
                 jax.ShapeDtypeStruct((SUBLANES, MLA_HEADS, t_all, 2 * LANES), BF16),
                 jax.ShapeDtypeStruct((SUBLANES, MLA_HEADS, t_all, V_HEAD + LANES), BF16)]
    if first:
        out_specs.append(pl.BlockSpec((rows, d), lambda t: (t, 0)))
        out_shape.append(jax.ShapeDtypeStruct((n_rows, d), F32))
    return pl.pallas_call(
        kern,
        grid=(n_rows // rows,),
        in_specs=data_specs + [_layer_spec(c, layer) for c in stacked] + [tab_spec],
        out_specs=out_specs,
        out_shape=out_shape,
        scratch_shapes=[pltpu.VMEM((stage_tiles, rows, LANES), F32)],
        compiler_params=_params(1),
        name="mla_qkv",
    )(*data, *stacked, tabs)


def _attn_call(q, k, v):
    b, nh, t_all, dk = q.shape
    return pl.pallas_call(
        _attn_kernel,
        grid=(b, nh // ATTN_HEADS),
        in_specs=[pl.BlockSpec((1, ATTN_HEADS, t_all, dk), lambda bi, hi: (bi, hi, 0, 0)),
                  pl.BlockSpec((1, ATTN_HEADS, t_all, dk), lambda bi, hi: (bi, hi, 0, 0)),
                  pl.BlockSpec((1, ATTN_HEADS, t_all, v.shape[3]), lambda bi, hi: (bi, hi, 0, 0))],
        out_specs=pl.BlockSpec((1, t_all, ATTN_HEADS * V_HEAD), lambda bi, hi: (bi, 0, hi)),
        out_shape=jax.ShapeDtypeStruct((b, t_all, nh * V_HEAD), BF16),
        compiler_params=_params(2),
        name="mla_attention",
    )(q, k, v)


def _rope_tables(t_all):
    half = QK_ROPE // 2
    inv = ROPE_THETA ** (-jnp.arange(0, QK_ROPE, 2, dtype=F32) / QK_ROPE)
    ang = jnp.arange(t_all, dtype=F32)[:, None] * jnp.tile(inv, LANES // half)[None, :]
    c, s = jnp.cos(ang), jnp.sin(ang)
    lane = jnp.arange(LANES)
    q_tab = jnp.where(lane < QK_ROPE, c, s)
    k_sin = jnp.where((lane // half) % 2 == 0, -s, s)
    return jnp.concatenate([q_tab, c, k_sin], axis=1)


def _rotate_half(a):
    half = a.shape[-1] // 2
    return jnp.concatenate([-a[..., half:], a[..., :half]], axis=-1)


def _swap_halves(a):
    half = a.shape[-1] // 2
    return jnp.concatenate([a[..., half:], a[..., :half]], axis=-1)


def kernel(x, meta, norm_g, w_in, rg_conv_w, rg_conv_b, rg_wa, rg_ba, rg_wx, rg_bx, rg_lambda, rg_out,
           sc_conv_w, sc_out, mla_cq_g, mla_w_uq, mla_ckv_g, mla_w_uk, mla_w_uv, mla_qnorm_g,
           mla_knorm_g, mla_out, gate_b, w_out):
    b, seq, d = x.shape
    depth = norm_g.shape[0]
    d_rnn = rg_out.shape[1]
    d_sc = sc_out.shape[1]
    t_all = N_META + seq
    assert b == SUBLANES, "time-major layout needs the batch to fill one f32 sublane tile"
    assert t_all % STEPS_PER_TILE == 0

    tabs = _rope_tables(t_all)

    o_sc = 2 * d_rnn
    o_cq = o_sc + 4 * d_sc
    o_kr = o_cq + Q_LORA + KV_LORA
    o_ga = o_kr + QK_ROPE
    n_grp = MLA_HEADS // HEAD_GROUP

    row = lambda a: a.reshape(depth, 1, -1)
    w_t = jnp.swapaxes(w_in, 1, 2)
    w_all = w_t.astype(BF16)
    w_kr = w_all[:, o_kr:o_ga]
    w_lat = jnp.concatenate([w_all[:, o_cq:o_kr], w_kr, w_kr], axis=1)

    wq = mla_w_uq.reshape(depth, Q_LORA, MLA_HEADS, QK_HEAD)
    wq_rope = wq[..., QK_NOPE:]
    wq_heads = jnp.concatenate([wq[..., :QK_NOPE], wq_rope, _rotate_half(wq_rope)], axis=3)
    wq_grp = wq_heads.reshape(depth, Q_LORA, n_grp, HEAD_GROUP * 2 * LANES).transpose(0, 2, 1, 3).astype(BF16)
    wk = mla_w_uk.reshape(depth, KV_LORA, MLA_HEADS, QK_NOPE)
    wv = mla_w_uv.reshape(depth, KV_LORA, MLA_HEADS, V_HEAD)
    wkv_grp = jnp.concatenate([wk, wv], axis=3).reshape(
        depth, KV_LORA, n_grp, HEAD_GROUP * 2 * LANES).transpose(0, 2, 1, 3).astype(BF16)
    qg_rope, kg_rope = mla_qnorm_g[:, QK_NOPE:], mla_knorm_g[:, QK_NOPE:]

    ng = row(norm_g)
    qk_const = QK_HEAD ** 0.5 * LOG2_E
    qkv_params = (ng, w_lat, row(mla_cq_g), row(mla_ckv_g), wq_grp, wkv_grp,
                  row(mla_qnorm_g[:, :QK_NOPE] * mla_knorm_g[:, :QK_NOPE] * qk_const),
                  row(jnp.concatenate([qg_rope, _swap_halves(qg_rope)], axis=1) * qk_const),
                  row(jnp.concatenate([kg_rope, kg_rope], axis=1)))
    mixer_params = (ng, rg_conv_w, row(rg_conv_b), jnp.concatenate([rg_wa, rg_wx], axis=-1).astype(BF16),
                    row(rg_ba), row(rg_bx), row(rg_lambda), rg_out.astype(BF16), sc_conv_w, sc_out.astype(BF16),
                    row(gate_b), mla_out.astype(BF16), w_out.astype(BF16))

    for l in range(depth):
        if l == 0:
            q, k, v, h = _qkv_call((x, meta.astype(x.dtype)), l, qkv_params, tabs)
        else:
            q, k, v = _qkv_call(h, l, qkv_params, tabs)
        o = _attn_call(q, k, v)
        h = _mixer_call(h, o, l, w_all, (o_ga, w_all.shape[1] - o_ga), mixer_params,
                        final_seq=seq if l == depth - 1 else None)
    return h
```

```python
import functools

import jax
import jax.numpy as jnp
from jax import lax
from jax.experimental import pallas as pl
from jax.experimental.pallas import tpu as pltpu

N_META = 16
EPS = 1e-6
RNN_BLOCKS = 4
RNN_CONV = 4
LRU_C = 8.0
SC_CONV = 3
MLA_HEADS = 8
QK_NOPE = 128
QK_ROPE = 64
QK_HEAD = QK_NOPE + QK_ROPE
V_HEAD = 128
Q_LORA = 384
KV_LORA = 256
ROPE_THETA = 10000.0
LOG2_E = 1.4426950408889634

LANES = 128
SUBLANES = 8
STEPS_PER_TILE = 48
ATTN_TILE = 256
ATTN_MACRO = 512
ATTN_HEADS = 2
HEAD_GROUP = 2
W_BLOCK = 2048
VMEM_LIMIT = 56 * 1024 * 1024
MIXER_VMEM_LIMIT = 60 * 1024 * 1024

F32 = jnp.float32
BF16 = jnp.bfloat16


def _rms(x, g):
    ms = jnp.mean(x * x, axis=-1, keepdims=True)
    return x * lax.rsqrt(ms + EPS) * g


def _silu(x):
    return x * jax.nn.sigmoid(x)


def _dot_t(a, w_t):
    return lax.dot_general(a, w_t, (((1,), (1,)), ((), ())), preferred_element_type=F32)


def _causal_taps(buf, cur, taps, halo, rows):
    width = len(taps)
    acc = taps[width - 1] * cur
    for k in range(width - 1):
        back = (width - 1 - k) * SUBLANES
        acc = acc + taps[k] * buf[pl.ds(halo - back, rows), :]
    return acc


def _mixer_kernel(x_ref, o_ref, ng_ref, wr_ref, ws0_ref, ws1_ref, wm_ref,
                  rcw_ref, rcb_ref, wax_ref, ba_ref, bx_ref, lam_ref, wro_ref,
                  scw_ref, wso_ref, gb_ref, wao_ref, wout_ref,
                  out_ref, xbuf, h_buf, h_carry, cbuf, o_stage, *out_scratch, n_meta=None):
    rows, d = x_ref.shape
    steps = o_ref.shape[1]
    d_rnn = wro_ref.shape[0]
    d_sc = wso_ref.shape[0]
    blk = d_rnn // RNN_BLOCKS
    rnn_halo = (RNN_CONV - 1) * SUBLANES
    sc_halo = (SC_CONV - 1) * SUBLANES

    @pl.when(pl.program_id(0) == 0)
    def _():
        xbuf[pl.ds(0, rnn_halo), :] = jnp.zeros((rnn_halo, d_rnn), F32)
        cbuf[pl.ds(0, sc_halo), :] = jnp.zeros((sc_halo, d_sc), F32)
        h_carry[...] = jnp.zeros_like(h_carry)

    for b in range(SUBLANES):
        ob = o_ref[b].astype(F32)
        for j in range(o_stage.shape[0]):
            o_stage[j, pl.ds(b, steps, stride=SUBLANES), :] = ob[:, j * LANES:(j + 1) * LANES]

    x = x_ref[...]
    h = _rms(x, ng_ref[...]).astype(BF16)

    zr = _dot_t(h, wr_ref[...])
    zs0 = _dot_t(h, ws0_ref[...])
    xr = zr[:, :d_rnn]
    xbuf[pl.ds(rnn_halo, rows), :] = xr
    taps = [rcw_ref[pl.ds(k, 1), :] for k in range(RNN_CONV)]
    xc = _causal_taps(xbuf, xr, taps, rnn_halo, rows) + rcb_ref[...]
    xbuf[pl.ds(0, rnn_halo), :] = xbuf[pl.ds(rows, rnn_halo), :]
    xcb = xc.astype(BF16)
    ra, ri = [], []
    for n in range(RNN_BLOCKS):
        xn = xcb[:, n * blk:(n + 1) * blk]
        gate_n = jnp.dot(xn, wax_ref[n], preferred_element_type=F32)
        ra.append(gate_n[:, :blk])
        ri.append(gate_n[:, blk:])

    zs1 = _dot_t(h, ws1_ref[...])

    zm = _dot_t(h, wm_ref[...])

    lam = lam_ref[...]
    softplus_neg = jnp.maximum(-lam, 0.0) + jnp.log1p(jnp.exp(-jnp.abs(lam)))
    for n in range(RNN_BLOCKS):
        cols = pl.ds(n * blk, blk)
        lo, hi = n * blk, (n + 1) * blk
        r = jax.nn.sigmoid(ra[n] + ba_ref[:, cols])
        i = jax.nn.sigmoid(ri[n] + bx_ref[:, cols])
        a = jnp.exp((-LRU_C) * r * softplus_neg[:, lo:hi])
        u = jnp.sqrt(1.0 - a * a) * (i * xc[:, lo:hi])
        hc = h_carry[:, cols]
        for t in range(rows // SUBLANES):
            sl = slice(t * SUBLANES, (t + 1) * SUBLANES)
            hc = a[sl] * hc + u[sl]
            h_buf[sl, cols] = hc
        h_carry[:, cols] = hc

    cx = zs0[:, d_sc:] * zs1[:, :d_sc]
    cbuf[pl.ds(sc_halo, rows), :] = cx
    staps = [scw_ref[pl.ds(k, 1), :] for k in range(SC_CONV)]
    conv = _causal_taps(cbuf, cx, staps, sc_halo, rows)
    cbuf[pl.ds(0, sc_halo), :] = cbuf[pl.ds(rows, sc_halo), :]
    y_sc = jnp.dot((zs0[:, :d_sc] * conv * _silu(zs1[:, d_sc:])).astype(BF16), wso_ref[...],
                   preferred_element_type=F32)

    y_rnn = jnp.dot((h_buf[...] * _silu(zr[:, d_rnn:])).astype(BF16), wro_ref[...],
                    preferred_element_type=F32)

    o_tm = jnp.concatenate([o_stage[j] for j in range(o_stage.shape[0])], axis=1)
    y_att = jnp.dot((o_tm * _silu(zm[:, :d])).astype(BF16), wao_ref[...], preferred_element_type=F32)

    gates = jax.nn.sigmoid(zm[:, d:] + gb_ref[...])
    merged = gates[:, :d] * y_rnn + gates[:, d:2 * d] * y_sc + gates[:, 2 * d:] * y_att
    res = x + jnp.dot(merged.astype(BF16), wout_ref[...], preferred_element_type=F32)
    if n_meta is None:
        out_ref[...] = res
    else:
        _store_batch_major(res, out_ref, o_stage, *out_scratch, n_meta)


def _store_batch_major(res, out_hbm, stage, out_stage, sems, n_meta):
    t = pl.program_id(0)
    nb, steps, d = out_stage.shape[1:]
    slot = lax.rem(t, 2)

    def tile_copy(slot_, t_):
        first_row = pl.multiple_of(t_ * steps - n_meta, n_meta)
        return pltpu.make_async_copy(out_stage.at[slot_], out_hbm.at[:, pl.ds(first_row, steps), :],
                                     sems.at[slot_])

    @pl.when(t >= 3)
    def _():
        tile_copy(slot, t - 2).wait()

    for j in range(stage.shape[0]):
        stage[j] = res[:, j * LANES:(j + 1) * LANES]
    for b in range(nb):
        out_stage[slot, b] = jnp.concatenate(
            [stage[j, pl.ds(b, steps, stride=nb), :] for j in range(stage.shape[0])], axis=1)

    @pl.when(t == 0)
    def _():
        head = pltpu.make_async_copy(out_stage.at[0, :, pl.ds(n_meta, steps - n_meta), :],
                                     out_hbm.at[:, pl.ds(0, steps - n_meta), :], sems.at[0])
        head.start()
        head.wait()

    @pl.when(t > 0)
    def _():
        tile_copy(slot, t).start()

    @pl.when(t == pl.num_programs(0) - 1)
    def _():
        tile_copy(1 - slot, t - 1).wait()
        tile_copy(slot, t).wait()


def _per_batch_rows(buf, steps):
    return jnp.concatenate(
        [jnp.concatenate([buf[j, pl.ds(b, steps, stride=SUBLANES), :] for j in range(buf.shape[0])], axis=1)
         for b in range(SUBLANES)], axis=0)


def _qkv_kernel(x_ref, ng_ref, w_ref, cqg_ref, ckvg_ref, wq_ref, wkv_ref, qgn_ref, qgr_ref,
                kgr_ref, tab_ref, q_ref, k_ref, v_ref, lat_buf):
    steps = x_ref.shape[0] // SUBLANES
    h = _rms(x_ref[...], ng_ref[...]).astype(BF16)
    z = _dot_t(h, w_ref[...])
    for j in range(lat_buf.shape[0]):
        lat_buf[j] = z[:, j * LANES:(j + 1) * LANES]
    _qkv_heads(_per_batch_rows(lat_buf, steps), steps, cqg_ref, ckvg_ref, wq_ref, wkv_ref, qgn_ref, qgr_ref,
               kgr_ref, tab_ref, q_ref, k_ref, v_ref)


def _qkv_first_kernel(xa_ref, xb_ref, xc_ref, meta_ref, ng_ref, w_ref, cqg_ref, ckvg_ref, wq_ref, wkv_ref,
                      qgn_ref, qgr_ref, kgr_ref, tab_ref, q_ref, k_ref, v_ref, h0_ref, stage):
    nb, part_steps, d = xa_ref.shape
    steps = 3 * part_steps
    first = pl.program_id(0) == 0
    per_batch = []
    for b in range(nb):
        xa = jnp.where(first, meta_ref[...], xa_ref[b])
        xb = jnp.concatenate([xa, xb_ref[b], xc_ref[b]], axis=0)
        for j in range(stage.shape[0]):
            stage[j, pl.ds(b, steps, stride=nb), :] = xb[:, j * LANES:(j + 1) * LANES]
        per_batch.append(xb)
    h0_ref[...] = jnp.concatenate([stage[j] for j in range(stage.shape[0])], axis=1)
    h = _rms(jnp.concatenate(per_batch, axis=0), ng_ref[...]).astype(BF16)
    _qkv_heads(_dot_t(h, w_ref[...]), steps, cqg_ref, ckvg_ref, wq_ref, wkv_ref, qgn_ref, qgr_ref,
               kgr_ref, tab_ref, q_ref, k_ref, v_ref)


def _qkv_heads(zb, steps, cqg_ref, ckvg_ref, wq_ref, wkv_ref, qgn_ref, qgr_ref, kgr_ref, tab_ref,
               q_ref, k_ref, v_ref):
    n_lat = Q_LORA + KV_LORA
    norm_eps = QK_HEAD * EPS
    ones = jnp.ones((SUBLANES, steps, LANES), v_ref.dtype)
    kr = zb[:, n_lat:]
    cq = _rms(zb[:, :Q_LORA], cqg_ref[...]).astype(BF16)
    ckv = _rms(zb[:, Q_LORA:n_lat], ckvg_ref[...]).astype(BF16)

    def tiled(j):
        return jnp.concatenate([tab_ref[:, j * LANES:(j + 1) * LANES]] * SUBLANES, axis=0)

    q_tab = tiled(0) * qgr_ref[...]
    kr_half_sq = 0.5 * kr * kr
    krg = kr * kgr_ref[...]
    kr_rot = krg * tiled(1) + pltpu.roll(krg, QK_ROPE // 2, axis=1) * tiled(2)

    def per_batch(a):
        return a.reshape(SUBLANES, steps, a.shape[-1])

    for grp in range(MLA_HEADS // HEAD_GROUP):
        qg = jnp.dot(cq, wq_ref[grp], preferred_element_type=F32)
        kvg = jnp.dot(ckv, wkv_ref[grp], preferred_element_type=F32)
        for j in range(HEAD_GROUP):
            hd = grp * HEAD_GROUP + j
            qn = qg[:, (2 * j) * LANES:(2 * j + 1) * LANES]
            qx = qg[:, (2 * j + 1) * LANES:(2 * j + 2) * LANES]
            q_ss = jnp.sum(qn * qn + 0.5 * (qx * qx), axis=-1, keepdims=True)
            q_inv = lax.rsqrt(q_ss + norm_eps)
            q_ref[:, hd, :, pl.ds(0, QK_NOPE)] = per_batch(qn * q_inv * qgn_ref[...]).astype(q_ref.dtype)
            q_ref[:, hd, :, pl.ds(QK_NOPE, LANES)] = per_batch(qx * q_tab * q_inv).astype(q_ref.dtype)

            kn = kvg[:, (2 * j) * LANES:(2 * j + 1) * LANES]
            k_ss = jnp.sum(kn * kn + kr_half_sq, axis=-1, keepdims=True)
            k_inv = lax.rsqrt(k_ss + norm_eps)
            k_ref[:, hd, :, pl.ds(0, QK_NOPE)] = per_batch(kn * k_inv).astype(k_ref.dtype)
            k_ref[:, hd, :, pl.ds(QK_NOPE, LANES)] = per_batch(kr_rot * k_inv).astype(k_ref.dtype)
            v_ref[:, hd, :, pl.ds(0, V_HEAD)] = per_batch(
                kvg[:, (2 * j + 1) * LANES:(2 * j + 2) * LANES]).astype(v_ref.dtype)
            v_ref[:, hd, :, pl.ds(V_HEAD, LANES)] = ones


def _attn_kernel(q_ref, k_ref, v_ref, o_ref):
    t_all = q_ref.shape[2]
    nt = (((1,), (1,)), ((), ()))

    def scores(hd, start, size):
        def keys(lo, n):
            return k_ref[0, hd, pl.ds(lo, n), :]

        q = q_ref[0, hd, pl.ds(start, size), :]
        s_off = lax.dot_general(q, keys(0, start), nt, preferred_element_type=F32) if start else None
        sub = min(size, ATTN_TILE)
        row = lax.broadcasted_iota(jnp.int32, (sub, sub), 0)
        col = lax.broadcasted_iota(jnp.int32, (sub, sub), 1)
        near = []
        for r in range(0, size, sub):
            qr = q[r:r + sub]
            s_in = lax.dot_general(qr, keys(start, r), nt, preferred_element_type=F32) if r else None
            s_dg = lax.dot_general(qr, keys(start + r, sub), nt, preferred_element_type=F32)
            near.append((s_in, jnp.where(col <= row, s_dg, -1e30)))
        return s_off, near

    def finish(hd, start, size, s_off, near):
        def vals(lo, n):
            return v_ref[0, hd, pl.ds(lo, n), :]

        sub = min(size, ATTN_TILE)
        maxes, p_offs = [], []
        for n, (s_in, s_dg) in enumerate(near):
            m = jnp.max(s_dg, axis=-1, keepdims=True)
            if s_in is not None:
                m = jnp.maximum(m, jnp.max(s_in, axis=-1, keepdims=True))
            if s_off is not None:
                so = s_off[n * sub:(n + 1) * sub]
                m = jnp.maximum(m, jnp.max(so, axis=-1, keepdims=True))
                p_offs.append(jnp.exp2(so - m).astype(BF16))
            maxes.append(m)
        if s_off is not None:
            p_off = jnp.concatenate(p_offs, axis=0) if len(p_offs) > 1 else p_offs[0]
            acc_off = jnp.dot(p_off, vals(0, start), preferred_element_type=F32)
        for n, (s_in, s_dg) in enumerate(near):
            r = n * sub
            m = maxes[n]
            acc = jnp.dot(jnp.exp2(s_dg - m).astype(BF16), vals(start + r, sub), preferred_element_type=F32)
            if s_in is not None:
                acc = acc + jnp.dot(jnp.exp2(s_in - m).astype(BF16), vals(start, r),
                                    preferred_element_type=F32)
            if s_off is not None:
                acc = acc + acc_off[r:r + sub]
            o_ref[0, pl.ds(start + r, sub), pl.ds(hd * V_HEAD, V_HEAD)] = (
                acc[:, :V_HEAD] / acc[:, V_HEAD:]).astype(o_ref.dtype)

    tiles = [(s, ATTN_MACRO) for s in range(0, t_all - ATTN_MACRO + 1, ATTN_MACRO)]
    done = len(tiles) * ATTN_MACRO
    if done < t_all:
        tiles.append((done, t_all - done))
    heads = range(q_ref.shape[1])
    pending = [scores(hd, *tiles[0]) for hd in heads]
    for idx, (start, size) in enumerate(tiles):
        nxt = [scores(hd, *tiles[idx + 1]) for hd in heads] if idx + 1 < len(tiles) else None
        for hd in heads:
            finish(hd, start, size, *pending[hd])
        pending = nxt


def _const_spec(shape):
    nd = len(shape)
    return pl.BlockSpec(shape, lambda *_: (0,) * nd)


def _layer_spec(arr, layer):
    nd = arr.ndim - 1
    return pl.BlockSpec((None,) + arr.shape[1:], lambda *_: (layer,) + (0,) * nd)


def _params(n_axes, vmem_limit=VMEM_LIMIT):
    return pltpu.CompilerParams(dimension_semantics=("arbitrary",) * n_axes, vmem_limit_bytes=vmem_limit)


def _tile_rows():
    return STEPS_PER_TILE * SUBLANES


def _mixer_call(h, o, layer, w_all, gate_rows, stacked, final_seq=None):
    n_rows, d = h.shape
    ng, rcw, rcb, wax, ba, bx, lam, wro, scw, wso, gb, wao, wout = stacked
    d_rnn = wro.shape[1]
    d_sc = wso.shape[1]
    assert 2 * d_rnn == W_BLOCK and 2 * d_sc == W_BLOCK, "rnn | conv column groups must be whole W_BLOCKs"
    rows = _tile_rows()
    row_spec = pl.BlockSpec((rows, d), lambda t: (t, 0))
    o_spec = pl.BlockSpec((SUBLANES, STEPS_PER_TILE, d), lambda t: (0, t, 0))

    def w_col_block(j):
        return pl.BlockSpec((None, W_BLOCK, w_all.shape[2]), lambda t: (layer, j, 0))

    gate_lo, gate_n = gate_rows
    wm_spec = pl.BlockSpec((pl.Squeezed(), pl.Element(gate_n), pl.Element(w_all.shape[2])),
                           lambda t: (layer, gate_lo, 0))
    tail = (rcw, rcb, wax, ba, bx, lam, wro, scw, wso, gb, wao, wout)
    if final_seq is None:
        kern, out_spec, out_scratch = _mixer_kernel, row_spec, []
        out_shape = jax.ShapeDtypeStruct((n_rows, d), F32)
    else:
        n_meta = n_rows // SUBLANES - final_seq
        assert 0 < n_meta < STEPS_PER_TILE and n_meta % SUBLANES == 0 and n_rows // rows > 3
        kern = functools.partial(_mixer_kernel, n_meta=n_meta)
        out_spec = pl.BlockSpec(memory_space=pl.ANY)
        out_shape = jax.ShapeDtypeStruct((SUBLANES, final_seq, d), F32)
        out_scratch = [pltpu.VMEM((2, SUBLANES, STEPS_PER_TILE, d), F32), pltpu.SemaphoreType.DMA((2,))]
    return pl.pallas_call(
        kern,
        grid=(n_rows // rows,),
        in_specs=[row_spec, o_spec, _layer_spec(ng, layer), w_col_block(0), w_col_block(1), w_col_block(2),
                  wm_spec] + [_layer_spec(c, layer) for c in tail],
        out_specs=out_spec,
        out_shape=out_shape,
        scratch_shapes=[pltpu.VMEM((rows + (RNN_CONV - 1) * SUBLANES, d_rnn), F32),
                        pltpu.VMEM((rows, d_rnn), F32), pltpu.VMEM((SUBLANES, d_rnn), F32),
                        pltpu.VMEM((rows + (SC_CONV - 1) * SUBLANES, d_sc), F32),
                        pltpu.VMEM((d // LANES, rows, LANES), F32)] + out_scratch,
        compiler_params=_params(1, MIXER_VMEM_LIMIT),
        name="mixers_merge",
    )(h, o, ng, w_all, w_all, w_all, w_all, *tail)


def _qkv_call(src, layer, stacked, tabs):
    rows = _tile_rows()
    first = isinstance(src, tuple)
    if first:
        x, meta = src
        nb, seq, d = x.shape
        part = STEPS_PER_TILE // 3
        assert meta.shape[0] == part and STEPS_PER_TILE == 3 * part and seq % part == 0 and nb == SUBLANES
        t_all = seq + part
        n_rows = t_all * nb

        def x_block(shift):
            return pl.BlockSpec((nb, part, d), lambda t: (0, jnp.maximum(3 * t + shift, 0), 0))

        kern = _qkv_first_kernel
        data_specs = [x_block(-1), x_block(0), x_block(1), _const_spec(meta.shape)]
        data = (x, x, x, meta)
        stage_tiles = d // LANES
    else:
        n_rows, d = src.shape
        t_all = n_rows // SUBLANES
        kern = _qkv_kernel
        data_specs = [pl.BlockSpec((rows, d), lambda t: (t, 0))]
        data = (src,)
        stage_tiles = stacked[1].shape[1] // LANES
    tab_spec = pl.BlockSpec((STEPS_PER_TILE, tabs.shape[1]), lambda t: (t, 0))
    qk_spec = pl.BlockSpec((SUBLANES, MLA_HEADS, STEPS_PER_TILE, 2 * LANES), lambda t: (0, 0, t, 0))
    v_spec = pl.BlockSpec((SUBLANES, MLA_HEADS, STEPS_PER_TILE, V_HEAD + LANES), lambda t: (0, 0, t, 0))
    out_specs = [qk_spec, qk_spec, v_spec]
    out_shape = [jax.ShapeDtypeStruct((SUBLANES, MLA_HEADS, t_all, 2 * LANES), BF16),
                 jax.ShapeDtypeStruct((SUBLANES, MLA_HEADS, t_all, 2 * LANES), BF16),
                 jax.ShapeDtypeStruct((SUBLANES, MLA_HEADS, t_all, V_HEAD + LANES), BF16)]
    if first:
        out_specs.append(pl.BlockSpec((rows, d), lambda t: (t, 0)))
        out_shape.append(jax.ShapeDtypeStruct((n_rows, d), F32))
    return pl.pallas_call(
        kern,
        grid=(n_rows // rows,),
        in_specs=data_specs + [_layer_spec(c, layer) for c in stacked] + [tab_spec],
        out_specs=out_specs,
        out_shape=out_shape,
        scratch_shapes=[pltpu.VMEM((stage_tiles, rows, LANES), F32)],
        compiler_params=_params(1),
        name="mla_qkv",
    )(*data, *stacked, tabs)


def _attn_call(q, k, v):
    b, nh, t_all, dk = q.shape
    return pl.pallas_call(
        _attn_kernel,
        grid=(b, nh // ATTN_HEADS),
        in_specs=[pl.BlockSpec((1, ATTN_HEADS, t_all, dk), lambda bi, hi: (bi, hi, 0, 0)),
                  pl.BlockSpec((1, ATTN_HEADS, t_all, dk), lambda bi, hi: (bi, hi, 0, 0)),
                  pl.BlockSpec((1, ATTN_HEADS, t_all, v.shape[3]), lambda bi, hi: (bi, hi, 0, 0))],
        out_specs=pl.BlockSpec((1, t_all, ATTN_HEADS * V_HEAD), lambda bi, hi: (bi, 0, hi)),
        out_shape=jax.ShapeDtypeStruct((b, t_all, nh * V_HEAD), BF16),
        compiler_params=_params(2),
        name="mla_attention",
    )(q, k, v)


def _rope_tables(t_all):
    half = QK_ROPE // 2
    inv = ROPE_THETA ** (-jnp.arange(0, QK_ROPE, 2, dtype=F32) / QK_ROPE)
    ang = jnp.arange(t_all, dtype=F32)[:, None] * jnp.tile(inv, LANES // half)[None, :]
    c, s = jnp.cos(ang), jnp.sin(ang)
    lane = jnp.arange(LANES)
    q_tab = jnp.where(lane < QK_ROPE, c, s)
    k_sin = jnp.where((lane // half) % 2 == 0, -s, s)
    return jnp.concatenate([q_tab, c, k_sin], axis=1)


def _rotate_half(a):
    half = a.shape[-1] // 2
    return jnp.concatenate([-a[..., half:], a[..., :half]], axis=-1)


def _swap_halves(a):
    half = a.shape[-1] // 2
    return jnp.concatenate([a[..., half:], a[..., :half]], axis=-1)


def kernel(x, meta, norm_g, w_in, rg_conv_w, rg_conv_b, rg_wa, rg_ba, rg_wx, rg_bx, rg_lambda, rg_out,
           sc_conv_w, sc_out, mla_cq_g, mla_w_uq, mla_ckv_g, mla_w_uk, mla_w_uv, mla_qnorm_g,
           mla_knorm_g, mla_out, gate_b, w_out):
    b, seq, d = x.shape
    depth = norm_g.shape[0]
    d_rnn = rg_out.shape[1]
    d_sc = sc_out.shape[1]
    t_all = N_META + seq
    assert b == SUBLANES, "time-major layout needs the batch to fill one f32 sublane tile"
    assert t_all % STEPS_PER_TILE == 0

    tabs = _rope_tables(t_all)

    o_sc = 2 * d_rnn
    o_cq = o_sc + 4 * d_sc
    o_kr = o_cq + Q_LORA + KV_LORA
    o_ga = o_kr + QK_ROPE
    n_grp = MLA_HEADS // HEAD_GROUP

    row = lambda a: a.reshape(depth, 1, -1)
    w_t = jnp.swapaxes(w_in, 1, 2)
    w_all = w_t.astype(BF16)
    w_kr = w_all[:, o_kr:o_ga]
    w_lat = jnp.concatenate([w_all[:, o_cq:o_kr], w_kr, w_kr], axis=1)

    wq = mla_w_uq.reshape(depth, Q_LORA, MLA_HEADS, QK_HEAD)
    wq_rope = wq[..., QK_NOPE:]
    wq_heads = jnp.concatenate([wq[..., :QK_NOPE], wq_rope, _rotate_half(wq_rope)], axis=3)
    wq_grp = wq_heads.reshape(depth, Q_LORA, n_grp, HEAD_GROUP * 2 * LANES).transpose(0, 2, 1, 3).astype(BF16)
    wk = mla_w_uk.reshape(depth, KV_LORA, MLA_HEADS, QK_NOPE)
    wv = mla_w_uv.reshape(depth, KV_LORA, MLA_HEADS, V_HEAD)
    wkv_grp = jnp.concatenate([wk, wv], axis=3).reshape(
        depth, KV_LORA, n_grp, HEAD_GROUP * 2 * LANES).transpose(0, 2, 1, 3).astype(BF16)
    qg_rope, kg_rope = mla_qnorm_g[:, QK_NOPE:], mla_knorm_g[:, QK_NOPE:]

    ng = row(norm_g)
    qk_const = QK_HEAD ** 0.5 * LOG2_E
    qkv_params = (ng, w_lat, row(mla_cq_g), row(mla_ckv_g), wq_grp, wkv_grp,
                  row(mla_qnorm_g[:, :QK_NOPE] * mla_knorm_g[:, :QK_NOPE] * qk_const),
                  row(jnp.concatenate([qg_rope, _swap_halves(qg_rope)], axis=1) * qk_const),
                  row(jnp.concatenate([kg_rope, kg_rope], axis=1)))
    mixer_params = (ng, rg_conv_w, row(rg_conv_b), jnp.concatenate([rg_wa, rg_wx], axis=-1).astype(BF16),
                    row(rg_ba), row(rg_bx), row(rg_lambda), rg_out.astype(BF16), sc_conv_w, sc_out.astype(BF16),
                    row(gate_b), mla_out.astype(BF16), w_out.astype(BF16))

    for l in range(depth):
        if l == 0:
            q, k, v, h = _qkv_call((x, meta.astype(x.dtype)), l, qkv_params, tabs)
        else:
            q, k, v = _qkv_call(h, l, qkv_params, tabs)
        o = _attn_call(q, k, v)
        h = _mixer_call(h, o, l, w_all, (o_ga, w_all.shape[1] - o_ga), mixer_params,
                        final_seq=seq if l == depth - 1 else None)
    return h
```
